```python
import math
import jax, jax.numpy as jnp
from jax import lax
import numpy as np

D_MODEL = 2048
BATCH = 4
SEQ = 4096
DEPTH = 2

CTX_LEN = 256
GRID_W = 64
NORM_EPS = 1e-6

HY_WIDTH = D_MODEL
HY_EMB = 33
HY_BANDS = (HY_EMB - 1) // 2
HY_FILTER_HIDDEN = 64
HY_FAST_DECAY_PCT = 0.3
HY_SLOW_DECAY_PCT = 1.5
HY_DECAY_TARGET = 1e-2

RET_HEADS = 8
RET_QK_DIM = D_MODEL // RET_HEADS
RET_V_DIM = 2 * D_MODEL // RET_HEADS
RET_QK_WIDTH = RET_HEADS * RET_QK_DIM
RET_V_WIDTH = RET_HEADS * RET_V_DIM
RET_CHUNK = 128
ROPE_BASE = 10000.0

SPLIT_SIZES = (RET_QK_WIDTH, RET_QK_WIDTH, RET_V_WIDTH, RET_V_WIDTH, 3 * HY_WIDTH, HY_WIDTH, 2 * D_MODEL)
IN_WIDTH = sum(SPLIT_SIZES)
ADA_WIDTH = 3 * D_MODEL

kernel_name = "hyena_retention_hybrid_dit"


def rmsnorm(x, g):
    xf = x.astype(jnp.float32)
    y = xf * lax.rsqrt(jnp.mean(xf * xf, axis=-1, keepdims=True) + NORM_EPS)
    return (y * g.astype(jnp.float32)).astype(x.dtype)


def ada_modulation(cond, ada_w, ada_b):
    mod = jax.nn.silu(cond) @ ada_w + ada_b
    return jnp.split(mod, 3, axis=-1)


def short_conv(u, w, b):
    L = u.shape[1]
    up = jnp.pad(u, ((0, 0), (1, 1), (0, 0)))
    return up[:, 0:L] * w[0] + up[:, 1:L + 1] * w[1] + up[:, 2:L + 2] * w[2] + b


def hyena_filters(L, w1, b1, w2, b2, w3, b3, freq, w_out):
    t = jnp.linspace(0.0, 1.0, L, dtype=jnp.float32)[:, None]
    ang = 2.0 * math.pi * jnp.arange(L, dtype=jnp.float32)[:, None] / L
    f = jnp.linspace(1e-4, HY_BANDS - 1, HY_BANDS, dtype=jnp.float32)[None, :]
    z = jnp.concatenate([t, jnp.cos(f * ang), -jnp.sin(f * ang)], axis=-1).astype(w1.dtype)
    h = jnp.sin(freq * (z @ w1 + b1))
    h = jnp.sin(freq * (h @ w2 + b2))
    h = jnp.sin(freq * (h @ w3 + b3))
    h = (h @ w_out).astype(jnp.float32)
    max_decay = math.log(HY_DECAY_TARGET) / HY_FAST_DECAY_PCT
    min_decay = math.log(HY_DECAY_TARGET) / HY_SLOW_DECAY_PCT
    deltas = jnp.abs(jnp.linspace(min_decay, max_decay, HY_WIDTH, dtype=jnp.float32))
    window = jnp.exp(-t * deltas[None, :])
    return h[:, :HY_WIDTH] * window, h[:, HY_WIDTH:] * window


def fft_long_conv(u, hf, hb):
    L, C = hf.shape
    k = jnp.concatenate([hf, jnp.zeros((1, C), jnp.float32), hb[1:][::-1]], axis=0)
    kf = jnp.fft.rfft(k, n=2 * L, axis=0)
    uf = jnp.fft.rfft(u.astype(jnp.float32), n=2 * L, axis=1)
    y = jnp.fft.irfft(uf * kf[None], n=2 * L, axis=1)[:, :L]
    return y.astype(u.dtype)


def to_heads(u, d):
    B, L, _ = u.shape
    return u.reshape(B, L, RET_HEADS, d).transpose(0, 2, 1, 3).astype(jnp.float32)


def rotary_2d(u, row, col):
    dh = u.shape[-1]
    half, quarter = dh // 2, dh // 4
    inv = 1.0 / (ROPE_BASE ** (jnp.arange(quarter, dtype=jnp.float32) / quarter))

    def rot(part, pos):
        a = pos[:, None] * inv[None, :]
        cos, sin = jnp.cos(a), jnp.sin(a)
        p1, p2 = part[..., :quarter], part[..., quarter:]
        return jnp.concatenate([p1 * cos - p2 * sin, p1 * sin + p2 * cos], axis=-1)

    return jnp.concatenate([rot(u[..., :half], row), rot(u[..., half:], col)], axis=-1)


def retention_scan(q, k, v, log_gamma, s0):
    B, H, L, _ = q.shape
    dv = v.shape[-1]
    n = L // RET_CHUNK

    def chunks(u):
        return u.reshape(B, H, n, RET_CHUNK, u.shape[-1]).transpose(2, 0, 1, 3, 4)

    idx = jnp.arange(RET_CHUNK, dtype=jnp.float32)
    lg = log_gamma.astype(jnp.float32)[:, None]
    diff = idx[:, None] - idx[None, :]
    inner_decay = jnp.where(diff >= 0, jnp.exp(lg[:, :, None] * jnp.maximum(diff, 0.0)), 0.0)
    q_decay = jnp.exp(lg * (idx + 1.0))[:, :, None]
    k_decay = jnp.exp(lg * (RET_CHUNK - 1.0 - idx))[:, :, None]
    chunk_decay = jnp.exp(lg * RET_CHUNK)[:, :, None]

    def step(s, blk):
        qi, ki, vi = blk
        scores = jnp.einsum('bhqd,bhkd->bhqk', qi, ki) * inner_decay
        o = jnp.einsum('bhqk,bhkv->bhqv', scores, vi) + jnp.einsum('bhqd,bhdv->bhqv', qi * q_decay, s)
        s = s * chunk_decay + jnp.einsum('bhkd,bhkv->bhdv', ki * k_decay, vi)
        return s, o

    s, o = lax.scan(step, s0, (chunks(q), chunks(k), chunks(v)))
    return o.transpose(1, 2, 0, 3, 4).reshape(B, H, L, dv), s


def head_rmsnorm(o):
    return o * lax.rsqrt(jnp.mean(o * o, axis=-1, keepdims=True) + NORM_EPS)


def mixer(h, rope_pos, s0_f, s0_b, w_in, conv_w, conv_b, f_w1, f_b1, f_w2, f_b2, f_w3, f_b3,
          f_freq, f_wout, hy_bias, log_decay, w_hy_out, w_ret_out, w_o):
    B, L, _ = h.shape
    points = np.cumsum(SPLIT_SIZES)[:-1].tolist()
    q, k, v, ret_gate, hy_in, hy_gate, merge_gate = jnp.split(h @ w_in, points, axis=-1)

    hf, hb = hyena_filters(L, f_w1, f_b1, f_w2, f_b2, f_w3, f_b3, f_freq, f_wout)
    hv, hx0, hx1 = jnp.split(short_conv(hy_in, conv_w, conv_b), 3, axis=-1)
    hv = hv * hx1
    hv = fft_long_conv(hv, hf, hb) + hy_bias * hv
    hy_y = hv * hx0

    q = to_heads(q, RET_QK_DIM)
    k = to_heads(k, RET_QK_DIM) * (RET_QK_DIM ** -0.5)
    v = to_heads(v, RET_V_DIM)
    if rope_pos is not None:
        q = rotary_2d(q, rope_pos[0], rope_pos[1])
        k = rotary_2d(k, rope_pos[0], rope_pos[1])
    o_f, s_f = retention_scan(q, k, v, log_decay[0], s0_f)
    o_b, s_b = retention_scan(jnp.flip(q, 2), jnp.flip(k, 2), jnp.flip(v, 2), log_decay[1], s0_b)
    o = head_rmsnorm(o_f + jnp.flip(o_b, 2))
    ret_y = o.transpose(0, 2, 1, 3).reshape(B, L, RET_V_WIDTH).astype(h.dtype)

    g_hy, g_ret = jnp.split(merge_gate, 2, axis=-1)
    hy_out = (hy_y * jax.nn.silu(hy_gate)) @ w_hy_out
    ret_out = (ret_y * jax.nn.silu(ret_gate)) @ w_ret_out
    out = (jax.nn.sigmoid(g_hy) * hy_out + jax.nn.sigmoid(g_ret) * ret_out) @ w_o
    return out, s_f, s_b


def context_states(h_ctx, w_in, log_decay):
    kv = h_ctx @ w_in[:, RET_QK_WIDTH:2 * RET_QK_WIDTH + RET_V_WIDTH]
    k, v = jnp.split(kv, [RET_QK_WIDTH], axis=-1)
    k = to_heads(k, RET_QK_DIM) * (RET_QK_DIM ** -0.5)
    v = to_heads(v, RET_V_DIM)
    L = k.shape[2]
    pos = jnp.arange(L, dtype=jnp.float32)
    w_f = jnp.exp(log_decay[0][:, None] * (L - 1.0 - pos))
    w_b = jnp.exp(log_decay[1][:, None] * pos)
    s_f = jnp.einsum('bhld,hl,bhlv->bhdv', k, w_f, v)
    s_b = jnp.einsum('bhld,hl,bhlv->bhdv', k, w_b, v)
    return s_f, s_b


def setup_inputs(seed: int = 0) -> dict:
    key = jax.random.key(seed)
    ks = jax.random.split(key, 24)
    f32 = jnp.float32

    def nrm(k, shape, s):
        return jax.random.normal(k, shape, f32) * s

    base_decay = np.log(-np.log(1.0 - 2.0 ** (-5.0 - np.arange(RET_HEADS)))).astype(np.float32)
    return {
        "x": nrm(ks[0], (BATCH, SEQ, D_MODEL), 1.0),
        "c": nrm(ks[1], (BATCH, D_MODEL), 1.0),
        "ctx": nrm(ks[2], (BATCH, CTX_LEN, D_MODEL), 1.0),
        "c_ctx": nrm(ks[3], (D_MODEL,), 1.0),
        "ln_g": 1.0 + nrm(ks[4], (DEPTH, D_MODEL), 0.02),
        "ada_w": nrm(ks[5], (DEPTH, D_MODEL, ADA_WIDTH), 0.5 * D_MODEL ** -0.5),
        "ada_b": nrm(ks[6], (DEPTH, ADA_WIDTH), 0.02),
        "w_in": nrm(ks[7], (DEPTH, D_MODEL, IN_WIDTH), D_MODEL ** -0.5),
        "hy_conv_w": nrm(ks[8], (DEPTH, 3, 3 * HY_WIDTH), 3.0 ** -0.5),
        "hy_conv_b": nrm(ks[9], (DEPTH, 3 * HY_WIDTH), 0.02),
        "hy_filt_w1": nrm(ks[10], (DEPTH, HY_EMB, HY_FILTER_HIDDEN), HY_EMB ** -0.5),
        "hy_filt_b1": nrm(ks[11], (DEPTH, HY_FILTER_HIDDEN), 0.02),
        "hy_filt_w2": nrm(ks[12], (DEPTH, HY_FILTER_HIDDEN, HY_FILTER_HIDDEN), HY_FILTER_HIDDEN ** -0.5),
        "hy_filt_b2": nrm(ks[13], (DEPTH, HY_FILTER_HIDDEN), 0.02),
        "hy_filt_w3": nrm(ks[14], (DEPTH, HY_FILTER_HIDDEN, HY_FILTER_HIDDEN), HY_FILTER_HIDDEN ** -0.5),
        "hy_filt_b3": nrm(ks[15], (DEPTH, HY_FILTER_HIDDEN), 0.02),
        "hy_filt_freq": 1.0 + nrm(ks[16], (DEPTH, HY_FILTER_HIDDEN), 0.02),
        "hy_filt_wout": nrm(ks[17], (DEPTH, HY_FILTER_HIDDEN, 2 * HY_WIDTH), 0.2 * HY_FILTER_HIDDEN ** -0.5),
        "hy_bias": nrm(ks[18], (DEPTH, HY_WIDTH), 0.1),
        "ret_decay": jnp.asarray(base_decay)[None, None, :] + nrm(ks[19], (DEPTH, 2, RET_HEADS), 0.01),
        "w_hy_out": nrm(ks[20], (DEPTH, HY_WIDTH, D_MODEL), HY_WIDTH ** -0.5),
        "w_ret_out": nrm(ks[21], (DEPTH, RET_V_WIDTH, D_MODEL), RET_V_WIDTH ** -0.5),
        "w_o": nrm(ks[22], (DEPTH, D_MODEL, D_MODEL), D_MODEL ** -0.5),
        "final_g": 1.0 + nrm(ks[23], (D_MODEL,), 0.02),
    }


def reference(x, c, ctx, c_ctx, ln_g, ada_w, ada_b, w_in, hy_conv_w, hy_conv_b, hy_filt_w1,
              hy_filt_b1, hy_filt_w2, hy_filt_b2, hy_filt_w3, hy_filt_b3, hy_filt_freq,
              hy_filt_wout, hy_bias, ret_decay, w_hy_out, w_ret_out, w_o, final_g):
    B, L, _ = x.shape
    rows = L // GRID_W
    row = jnp.repeat(jnp.arange(rows, dtype=jnp.float32), GRID_W)
    col = jnp.tile(jnp.arange(GRID_W, dtype=jnp.float32), rows)
    zero_state = jnp.zeros((B, RET_HEADS, RET_QK_DIM, RET_V_DIM), jnp.float32)

    for i in range(DEPTH):
        log_decay = -jnp.exp(ret_decay[i].astype(jnp.float32))
        mixer_params = (w_in[i], hy_conv_w[i], hy_conv_b[i], hy_filt_w1[i], hy_filt_b1[i],
                        hy_filt_w2[i], hy_filt_b2[i], hy_filt_w3[i], hy_filt_b3[i],
                        hy_filt_freq[i], hy_filt_wout[i], hy_bias[i], log_decay,
                        w_hy_out[i], w_ret_out[i], w_o[i])

        sh_c, sc_c, g_c = ada_modulation(c_ctx[None, None, :], ada_w[i], ada_b[i])
        h_ctx = rmsnorm(ctx, ln_g[i]) * (1.0 + sc_c) + sh_c
        if i < DEPTH - 1:
            out_c, s_ctx_f, s_ctx_b = mixer(h_ctx, None, zero_state, zero_state, *mixer_params)
            ctx_next = ctx + g_c * out_c
        else:
            s_ctx_f, s_ctx_b = context_states(h_ctx, w_in[i], log_decay)
            ctx_next = ctx

        sh, sc, g = ada_modulation(c[:, None, :], ada_w[i], ada_b[i])
        h = rmsnorm(x, ln_g[i]) * (1.0 + sc) + sh
        out, _, _ = mixer(h, (row, col), s_ctx_f, s_ctx_b, *mixer_params)
        x = x + g * out
        ctx = ctx_next

    return rmsnorm(x, final_g)
```

```python
import functools
import math

import numpy as np
import jax
import jax.numpy as jnp
from jax import lax
from jax.experimental import pallas as pl
from jax.experimental.pallas import tpu as pltpu

F32 = jnp.float32
BF16 = jnp.bfloat16

NORM_EPS = 1e-6
GRID_W = 64
ROPE_BASE = 10000.0
QK_DIM = 256
V_DIM = 512
RET_CHUNK = 256
HY_EMB = 33
HY_BANDS = (HY_EMB - 1) // 2
HY_EMB_PAD = 64
HY_FAST_DECAY_PCT = 0.3
HY_SLOW_DECAY_PCT = 1.5
HY_DECAY_TARGET = 1e-2
FFT_N2 = 64
VMEM_LIMIT = 56 * 1024 * 1024


def _cparams(sem):
    return pltpu.CompilerParams(dimension_semantics=sem, vmem_limit_bytes=VMEM_LIMIT)


def _silu(x):
    return x / (1.0 + jnp.exp(-x))


def _sigmoid(x):
    return 1.0 / (1.0 + jnp.exp(-x))


def _bdot(a, b):
    return jnp.dot(a, b, preferred_element_type=F32)


def _ada_kernel(c_ref, w_ref, b_ref, o_ref):
    s = _silu(c_ref[...]).astype(BF16)
    o_ref[0] = _bdot(s, w_ref[0].astype(BF16)) + b_ref[0]


def _ada_modulation(cond, ada_w, ada_b):
    depth, d, w3 = ada_w.shape
    r = cond.shape[0]
    tn = min(512, w3)
    return pl.pallas_call(
        _ada_kernel,
        grid=(depth, w3 // tn),
        in_specs=[
            pl.BlockSpec((r, d), lambda l, j: (0, 0)),
            pl.BlockSpec((1, d, tn), lambda l, j: (l, 0, j)),
            pl.BlockSpec((1, 1, tn), lambda l, j: (l, 0, j)),
        ],
        out_specs=pl.BlockSpec((1, r, tn), lambda l, j: (l, 0, j)),
        out_shape=jax.ShapeDtypeStruct((depth, r, w3), F32),
        compiler_params=_cparams(("arbitrary", "arbitrary")),
        name="ada_mod",
    )(cond, ada_w, ada_b.reshape(depth, 1, w3))


def _prenorm_kernel(x_ref, g_ref, sc_ref, sh_ref, o_ref):
    x = x_ref[0]
    y = x * lax.rsqrt(jnp.mean(x * x, axis=-1, keepdims=True) + NORM_EPS)
    o_ref[0] = ((y * g_ref[...]) * (1.0 + sc_ref[0]) + sh_ref[0]).astype(BF16)


def _prenorm(x, g, sc, sh):
    b, l, d = x.shape
    tl = min(512, l)
    return pl.pallas_call(
        _prenorm_kernel,
        grid=(b, l // tl),
        in_specs=[
            pl.BlockSpec((1, tl, d), lambda i, j: (i, j, 0)),
            pl.BlockSpec((1, d), lambda i, j: (0, 0)),
            pl.BlockSpec((1, 1, d), lambda i, j: (i, 0, 0)),
            pl.BlockSpec((1, 1, d), lambda i, j: (i, 0, 0)),
        ],
        out_specs=pl.BlockSpec((1, tl, d), lambda i, j: (i, j, 0)),
        out_shape=jax.ShapeDtypeStruct((b, l, d), BF16),
        compiler_params=_cparams(("arbitrary", "arbitrary")),
        name="prenorm",
    )(x, g.reshape(1, d), sc, sh)


def _inproj_kernel(*refs, r, use_rope, tn):
    if use_rope:
        h_ref, w_ref, cos_ref, sin_ref, o_ref = refs
    else:
        h_ref, w_ref, o_ref = refs
    j = pl.program_id(1)
    acc = _bdot(h_ref[...], w_ref[...])
    k_scale = QK_DIM ** -0.5

    def rope_store(scale):
        for g in range(tn // 128):
            slab = acc[:, g * 128:(g + 1) * 128]
            t = (g % 2) * 128
            rot = slab * cos_ref[:, t:t + 128] + pltpu.roll(slab, 64, 1) * sin_ref[:, t:t + 128]
            o_ref[:, g * 128:(g + 1) * 128] = (rot * scale).astype(BF16)

    @pl.when(j < r)
    def _():
        if use_rope:
            rope_store(1.0)
        else:
            o_ref[...] = acc.astype(BF16)

    @pl.when((j >= r) & (j < 2 * r))
    def _():
        if use_rope:
            rope_store(k_scale)
        else:
            o_ref[...] = (acc * k_scale).astype(BF16)

    @pl.when(((j >= 2 * r) & (j < 4 * r)) | ((j >= 6 * r) & (j < 9 * r)))
    def _():
        o_ref[...] = acc.astype(BF16)

    @pl.when(((j >= 4 * r) & (j < 6 * r)) | ((j >= 9 * r) & (j < 10 * r)))
    def _():
        o_ref[...] = _silu(acc).astype(BF16)

    @pl.when(j >= 10 * r)
    def _():
        o_ref[...] = _sigmoid(acc).astype(BF16)


def _rope_tables(l):
    quarter = QK_DIM // 4
    inv = 1.0 / (ROPE_BASE ** (jnp.arange(quarter, dtype=F32) / quarter))
    t = jnp.arange(l)
    row = (t // GRID_W).astype(F32)
    col = (t % GRID_W).astype(F32)
    ar = row[:, None] * inv[None, :]
    ac = col[:, None] * inv[None, :]
    cos_t = jnp.concatenate([jnp.cos(ar), jnp.cos(ar), jnp.cos(ac), jnp.cos(ac)], axis=-1)
    sin_t = jnp.concatenate([-jnp.sin(ar), jnp.sin(ar), -jnp.sin(ac), jnp.sin(ac)], axis=-1)
    return cos_t, sin_t


def _inproj(h, w, l, use_rope):
    m, d = h.shape
    nw = w.shape[1]
    tm = min(1024, l)
    tn = min(1024, d)
    r = d // tn
    in_specs = [
        pl.BlockSpec((tm, d), lambda i, j: (i, 0)),
        pl.BlockSpec((d, tn), lambda i, j: (0, j)),
    ]
    args = [h, w]
    if use_rope:
        cos_t, sin_t = _rope_tables(l)
        lb = l // tm
        in_specs += [pl.BlockSpec((tm, QK_DIM), lambda i, j: (i % lb, 0))] * 2
        args += [cos_t, sin_t]
    return pl.pallas_call(
        functools.partial(_inproj_kernel, r=r, use_rope=use_rope, tn=tn),
        grid=(m // tm, nw // tn),
        in_specs=in_specs,
        out_specs=pl.BlockSpec((tm, tn), lambda i, j: (i, j)),
        out_shape=jax.ShapeDtypeStruct((m, nw), BF16),
        compiler_params=_cparams(("arbitrary", "arbitrary")),
        name="in_proj",
    )(*args)


def _hyfront_kernel(v_ref, x0_ref, x1_ref, g_ref, wv_ref, w0_ref, w1_ref,
                    bv_ref, b0_ref, b1_ref, u_ref, m_ref, *, l):
    rows = lax.broadcasted_iota(jnp.int32, (l, 1), 0)

    def conv3(x_ref, w_ref, b_ref):
        x = x_ref[0].astype(F32)
        prev = jnp.where(rows == 0, 0.0, pltpu.roll(x, 1, 0))
        nxt = jnp.where(rows == l - 1, 0.0, pltpu.roll(x, l - 1, 0))
        return prev * w_ref[0:1, :] + x * w_ref[1:2, :] + nxt * w_ref[2:3, :] + b_ref[...]

    hv = conv3(v_ref, wv_ref, bv_ref)
    hx1 = conv3(x1_ref, w1_ref, b1_ref)
    u_ref[0] = (hv * hx1).astype(BF16)
    hx0 = conv3(x0_ref, w0_ref, b0_ref)
    m_ref[0] = (hx0 * g_ref[0].astype(F32)).astype(BF16)


def _hyena_front(p3, conv_w, conv_b, d):
    b, l, _ = p3.shape
    cs = 128
    nb = d // cs
    pspec = lambda off: pl.BlockSpec((1, l, cs), lambda i, j: (i, 0, off * nb + j))
    wspec = lambda off: pl.BlockSpec((3, cs), lambda i, j: (0, off * nb + j))
    bspec = lambda off: pl.BlockSpec((1, cs), lambda i, j: (0, off * nb + j))
    ospec = pl.BlockSpec((1, l, cs), lambda i, j: (i, 0, j))
    return pl.pallas_call(
        functools.partial(_hyfront_kernel, l=l),
        grid=(b, nb),
        in_specs=[pspec(6), pspec(7), pspec(8), pspec(9),
                  wspec(0), wspec(1), wspec(2), bspec(0), bspec(1), bspec(2)],
        out_specs=[ospec, ospec],
        out_shape=[jax.ShapeDtypeStruct((b, l, d), BF16)] * 2,
        compiler_params=_cparams(("arbitrary", "arbitrary")),
        name="hyena_front",
    )(p3, p3, p3, p3, conv_w, conv_w, conv_w,
      conv_b.reshape(1, -1), conv_b.reshape(1, -1), conv_b.reshape(1, -1))


def _filter_kernel(z_ref, t_ref, w1_ref, b1_ref, w2_ref, b2_ref, w3_ref, b3_ref, fr_ref,
                   wo_ref, dl_ref, o_ref, *, tr, l):
    dot = functools.partial(jnp.dot, precision=lax.Precision.HIGHEST, preferred_element_type=F32)
    f = fr_ref[...]
    h = jnp.sin(f * (dot(z_ref[...], w1_ref[...]) + b1_ref[...]))
    h = jnp.sin(f * (dot(h, w2_ref[...]) + b2_ref[...]))
    h = jnp.sin(f * (dot(h, w3_ref[...]) + b3_ref[...]))
    y = dot(h, wo_ref[...])
    win = jnp.exp(-t_ref[...] * dl_ref[...])
    rows = pl.program_id(0) * tr + lax.broadcasted_iota(jnp.int32, (tr, 1), 0)
    o_ref[...] = jnp.where(rows == l, 0.0, y * win).astype(BF16)


def _hyena_filter_taps(l, c, w1, b1, w2, b2, w3, b3, freq, w_out):
    t = jnp.linspace(0.0, 1.0, l, dtype=F32)[:, None]
    ang = 2.0 * math.pi * jnp.arange(l, dtype=F32)[:, None] / l
    f = jnp.linspace(1e-4, HY_BANDS - 1, HY_BANDS, dtype=F32)[None, :]
    z = jnp.concatenate([t, jnp.cos(f * ang), -jnp.sin(f * ang)], axis=-1)
    z = jnp.pad(z, ((0, 0), (0, HY_EMB_PAD - HY_EMB)))
    back = lambda a: jnp.concatenate([a[l - 1:l], a[:0:-1]], axis=0)
    z2 = jnp.concatenate([z, back(z)], axis=0)
    t2 = jnp.concatenate([t, back(t)], axis=0)
    max_decay = math.log(HY_DECAY_TARGET) / HY_FAST_DECAY_PCT
    min_decay = math.log(HY_DECAY_TARGET) / HY_SLOW_DECAY_PCT
    deltas = jnp.abs(jnp.linspace(min_decay, max_decay, c, dtype=F32))[None, :]
    w1p = jnp.pad(w1, ((0, HY_EMB_PAD - HY_EMB), (0, 0)))
    hid = w2.shape[0]
    tr = min(512, l)
    lb = l // tr
    full = lambda shape: pl.BlockSpec(shape, lambda i: (0, 0))
    return pl.pallas_call(
        functools.partial(_filter_kernel, tr=tr, l=l),
        grid=(2 * lb,),
        in_specs=[
            pl.BlockSpec((tr, HY_EMB_PAD), lambda i: (i, 0)),
            pl.BlockSpec((tr, 1), lambda i: (i, 0)),
            full((HY_EMB_PAD, hid)), full((1, hid)),
            full((hid, hid)), full((1, hid)),
            full((hid, hid)), full((1, hid)),
            full((1, hid)),
            pl.BlockSpec((hid, c), lambda i: (0, i // lb)),
            full((1, c)),
        ],
        out_specs=pl.BlockSpec((tr, c), lambda i: (i, 0)),
        out_shape=jax.ShapeDtypeStruct((2 * l, c), BF16),
        compiler_params=_cparams(("arbitrary",)),
        name="hyena_filter",
    )(z2, t2, w1p, b1.reshape(1, -1), w2, b2.reshape(1, -1), w3, b3.reshape(1, -1),
      freq.reshape(1, -1), w_out, deltas)


def _dft_consts(l, n2):
    n = 2 * l
    n1 = n // n2
    nin = n1 // 2

    def cs(k, m, period):
        ph = 2.0 * np.pi * ((np.outer(k, m)) % period) / period
        return np.cos(ph), np.sin(ph)

    k1 = np.arange(n1)
    c, s = cs(k1, np.arange(nin), n1)
    f1 = np.block([[c, s], [-s, c]])
    cf, sf = cs(k1, np.arange(n1), n1)
    f1_real = np.concatenate([cf, -sf], axis=0)
    ci, si = cs(np.arange(nin), k1, n1)
    g1 = np.block([[ci, -si], [si, ci]]) / n
    out = dict(n1=n1, n2=n2, nin=nin,
               f1=jnp.asarray(f1, F32).astype(BF16),
               f1_real=jnp.asarray(f1_real, F32).astype(BF16),
               g1=jnp.asarray(g1, F32).astype(BF16))
    if n2 > 1:
        m2 = np.arange(n2)
        ph = (m2[None, None, :] * k1[:, None, None] + n1 * m2[None, None, :] * m2[None, :, None]) % n
        ph = 2.0 * np.pi * ph / n
        tr, ti = np.cos(ph), -np.sin(ph)
        t_fwd = np.concatenate([np.concatenate([tr, -ti], axis=2),
                                np.concatenate([ti, tr], axis=2)], axis=1)
        trt, tit = tr.transpose(0, 2, 1), ti.transpose(0, 2, 1)
        t_inv = np.concatenate([np.concatenate([trt, tit], axis=2),
                                np.concatenate([-tit, trt], axis=2)], axis=1)
        out["t_fwd"] = jnp.asarray(t_fwd, F32).astype(BF16)
        out["t_inv"] = jnp.asarray(t_inv, F32).astype(BF16)
    return out


def _cmm_kernel(*refs, mode, chunk, nchunk):
    a_ref, x_ref, o_ref = refs[0], refs[1], refs[-1]
    a = a_ref[...]
    hm = a.shape[0] // 2
    for cc in range(nchunk):
        sl = slice(cc * chunk, (cc + 1) * chunk)
        acc = _bdot(a, x_ref[0, :, sl])
        if mode == "kfmul":
            kf_ref = refs[2]
            xr, xi = acc[:hm], acc[hm:]
            kr, ki = kf_ref[0, :, sl], kf_ref[1, :, sl]
            o_ref[0, :hm, sl] = (xr * kr - xi * ki).astype(o_ref.dtype)
            o_ref[0, hm:, sl] = (xr * ki + xi * kr).astype(o_ref.dtype)
        elif mode == "epi":
            u_ref, m_ref, b_ref = refs[2], refs[3], refs[4]
            u = u_ref[0, :, sl].astype(F32)
            o = (acc + b_ref[:, sl] * u) * m_ref[0, :, sl].astype(F32)
            o_ref[0, :, sl] = o.astype(o_ref.dtype)
        else:
            o_ref[0, :, sl] = acc.astype(o_ref.dtype)


def _cmm(a, x, mode="plain", extra=(), out_dtype=BF16):
    mr, k = a.shape
    g, _, cols = x.shape
    tc = min(8192, cols)
    chunk = min(1024, tc)
    in_specs = [pl.BlockSpec((mr, k), lambda i, j: (0, 0)),
                pl.BlockSpec((1, k, tc), lambda i, j: (i, 0, j))]
    if mode == "kfmul":
        in_specs += [pl.BlockSpec((2, mr // 2, tc), lambda i, j: (0, 0, j))]
    elif mode == "epi":
        in_specs += [pl.BlockSpec((1, mr, tc), lambda i, j: (i, 0, j)),
                     pl.BlockSpec((1, mr, tc), lambda i, j: (i, 0, j)),
                     pl.BlockSpec((1, tc), lambda i, j: (0, j))]
    return pl.pallas_call(
        functools.partial(_cmm_kernel, mode=mode, chunk=chunk, nchunk=tc // chunk),
        grid=(g, cols // tc),
        in_specs=in_specs,
        out_specs=pl.BlockSpec((1, mr, tc), lambda i, j: (i, 0, j)),
        out_shape=jax.ShapeDtypeStruct((g, mr, cols), out_dtype),
        compiler_params=_cparams(("arbitrary", "arbitrary")),
        name="dft_" + mode,
    )(a, x, *extra)


def _mid_kernel(a_ref, t_ref, kf_ref, ti_ref, o_ref, *, kb, n2):
    for kk in range(kb):
        a = jnp.concatenate([a_ref[0, 0, kk], a_ref[0, 1, kk]], axis=0)
        x = _bdot(t_ref[kk], a)
        xr, xi = x[:n2], x[n2:]
        kr, ki = kf_ref[0, kk], kf_ref[1, kk]
        y = jnp.concatenate([xr * kr - xi * ki, xr * ki + xi * kr], axis=0).astype(BF16)
        bm = _bdot(ti_ref[kk], y)
        o_ref[0, 0, kk] = bm[:n2].astype(BF16)
        o_ref[0, 1, kk] = bm[n2:].astype(BF16)


def _mid_filter_kernel(a_ref, t_ref, o_ref, *, kb, n2):
    for kk in range(kb):
        a = jnp.concatenate([a_ref[0, 0, kk], a_ref[0, 1, kk]], axis=0)
        x = _bdot(t_ref[kk], a)
        o_ref[0, kk] = x[:n2]
        o_ref[1, kk] = x[n2:]


def _mid_tiles(n1, c):
    return min(8, n1), min(512, c)


def _fft_mid(a5, kf, consts):
    p, _, n1, n2, c = a5.shape
    kb, cs = _mid_tiles(n1, c)
    aspec = pl.BlockSpec((1, 2, kb, n2, cs), lambda i, j, q: (q, 0, i, 0, j))
    tspec = pl.BlockSpec((kb, 2 * n2, 2 * n2), lambda i, j, q: (i, 0, 0))
    return pl.pallas_call(
        functools.partial(_mid_kernel, kb=kb, n2=n2),
        grid=(n1 // kb, c // cs, p),
        in_specs=[aspec, tspec,
                  pl.BlockSpec((2, kb, n2, cs), lambda i, j, q: (0, i, 0, j)),
                  tspec],
        out_specs=aspec,
        out_shape=jax.ShapeDtypeStruct(a5.shape, BF16),
        compiler_params=_cparams(("arbitrary", "arbitrary", "arbitrary")),
        name="dft_mid",
    )(a5, consts["t_fwd"], kf, consts["t_inv"])


def _fft_mid_filter(a5, consts):
    _, _, n1, n2, c = a5.shape
    kb, cs = _mid_tiles(n1, c)
    return pl.pallas_call(
        functools.partial(_mid_filter_kernel, kb=kb, n2=n2),
        grid=(n1 // kb, c // cs),
        in_specs=[pl.BlockSpec((1, 2, kb, n2, cs), lambda i, j: (0, 0, i, 0, j)),
                  pl.BlockSpec((kb, 2 * n2, 2 * n2), lambda i, j: (i, 0, 0))],
        out_specs=pl.BlockSpec((2, kb, n2, cs), lambda i, j: (0, i, 0, j)),
        out_shape=jax.ShapeDtypeStruct((2, n1, n2, c), F32),
        compiler_params=_cparams(("arbitrary", "arbitrary")),
        name="dft_mid_filter",
    )(a5, consts["t_fwd"])


def _filter_spectrum(taps, consts):
    n, c = taps.shape
    n1, n2 = consts["n1"], consts["n2"]
    x = taps.reshape(1, n1, n2 * c)
    if n2 == 1:
        return _cmm(consts["f1_real"], x, out_dtype=F32).reshape(2, n1, c)
    a = _cmm(consts["f1_real"], x)
    return _fft_mid_filter(a.reshape(1, 2, n1, n2, c), consts)


def _long_conv(u, m, bias, kf, consts):
    b, l, c = u.shape
    n1, n2, nin = consts["n1"], consts["n2"], consts["nin"]
    p = b // 2
    cols = n2 * c
    x = u.reshape(p, 2 * nin, cols)
    if n2 == 1:
        y = _cmm(consts["f1"], x, mode="kfmul", extra=(kf,))
    else:
        a = _cmm(consts["f1"], x)
        y = _fft_mid(a.reshape(p, 2, n1, n2, c), kf, consts).reshape(p, 2 * n1, cols)
    bias_t = jnp.tile(bias.reshape(1, c), (1, n2))
    out = _cmm(consts["g1"], y, mode="epi", extra=(x, m.reshape(p, 2 * nin, cols), bias_t))
    return out.reshape(b, l, c)


def _ret_kernel(rd_ref, q_ref, k_ref, v_ref, g_ref, s0f_ref, s0b_ref,
                o_ref, sf_ref, sb_ref, s_scr, sball_scr, dmask_scr, *, nch, cl):
    hd = pl.program_id(1)
    one = jnp.ones((1, 1), F32)
    lgf = -jnp.exp(one * rd_ref[0, hd])
    lgb = -jnp.exp(one * rd_ref[1, hd])
    ri = lax.broadcasted_iota(jnp.int32, (cl, 1), 0).astype(F32)
    ci = lax.broadcasted_iota(jnp.int32, (1, cl), 1).astype(F32)
    q_dec_f = jnp.exp(lgf * (ri + 1.0))
    q_dec_b = jnp.exp(lgb * (cl - ri))
    k_dec_f = jnp.exp(lgf * (cl - 1.0 - ci))
    k_dec_b = jnp.exp(lgb * ci)
    chunk_dec_f = jnp.exp(lgf * cl)
    chunk_dec_b = jnp.exp(lgb * cl)
    diff = ri - ci
    dmask_scr[...] = (jnp.where(diff >= 0, jnp.exp(lgf * jnp.maximum(diff, 0.0)), 0.0)
                      + jnp.where(diff <= 0, jnp.exp(lgb * jnp.maximum(-diff, 0.0)), 0.0))

    def k_transposed(k, dec):
        return (k.astype(F32).T * dec).astype(BF16)

    s_scr[...] = s0b_ref[0, 0]

    def bstep(t, carry):
        c = nch - 1 - t
        r0 = pl.multiple_of(c * cl, cl)
        sball_scr[c] = s_scr[...].astype(BF16)
        k = k_ref[0, pl.ds(r0, cl), :]
        v = v_ref[0, pl.ds(r0, cl), :]
        s_scr[...] = s_scr[...] * chunk_dec_b + _bdot(k_transposed(k, k_dec_b), v)
        return carry

    lax.fori_loop(0, nch, bstep, 0)
    sb_ref[0, 0] = s_scr[...]

    s_scr[...] = s0f_ref[0, 0]

    def fstep(c, carry):
        r0 = pl.multiple_of(c * cl, cl)
        q = q_ref[0, pl.ds(r0, cl), :]
        k = k_ref[0, pl.ds(r0, cl), :]
        v = v_ref[0, pl.ds(r0, cl), :]
        scores = lax.dot_general(q, k, (((1,), (1,)), ((), ())), preferred_element_type=F32)
        o = _bdot((scores * dmask_scr[...]).astype(BF16), v)
        qf = q.astype(F32)
        q2 = jnp.concatenate([(qf * q_dec_f).astype(BF16), (qf * q_dec_b).astype(BF16)], axis=1)
        s2 = jnp.concatenate([s_scr[...].astype(BF16), sball_scr[c]], axis=0)
        o = o + _bdot(q2, s2)
        o = o * lax.rsqrt(jnp.mean(o * o, axis=-1, keepdims=True) + NORM_EPS)
        o_ref[0, pl.ds(r0, cl), :] = (o * g_ref[0, pl.ds(r0, cl), :].astype(F32)).astype(BF16)
        s_scr[...] = s_scr[...] * chunk_dec_f + _bdot(k_transposed(k, k_dec_f), v)
        return carry

    lax.fori_loop(0, nch, fstep, 0)
    sf_ref[0, 0] = s_scr[...]


def _retention(p3, ret_decay, s0f, s0b, d):
    b, l, _ = p3.shape
    h = d // QK_DIM
    cl = min(RET_CHUNK, l)
    nch = l // cl
    sspec = pl.BlockSpec((1, 1, QK_DIM, V_DIM), lambda i, j: (i, j, 0, 0))
    return pl.pallas_call(
        functools.partial(_ret_kernel, nch=nch, cl=cl),
        grid=(b, h),
        in_specs=[
            pl.BlockSpec(memory_space=pltpu.SMEM),
            pl.BlockSpec((1, l, QK_DIM), lambda i, j: (i, 0, j)),
            pl.BlockSpec((1, l, QK_DIM), lambda i, j: (i, 0, h + j)),
            pl.BlockSpec((1, l, V_DIM), lambda i, j: (i, 0, h + j)),
            pl.BlockSpec((1, l, V_DIM), lambda i, j: (i, 0, 2 * h + j)),
            sspec, sspec,
        ],
        out_specs=[pl.BlockSpec((1, l, V_DIM), lambda i, j: (i, 0, j)), sspec, sspec],
        out_shape=[jax.ShapeDtypeStruct((b, l, h * V_DIM), BF16),
                   jax.ShapeDtypeStruct((b, h, QK_DIM, V_DIM), F32),
                   jax.ShapeDtypeStruct((b, h, QK_DIM, V_DIM), F32)],
        scratch_shapes=[pltpu.VMEM((QK_DIM, V_DIM), F32),
                        pltpu.VMEM((nch, QK_DIM, V_DIM), BF16),
                        pltpu.VMEM((cl, cl), F32)],
        compiler_params=_cparams(("arbitrary", "arbitrary")),
        name="retention",
    )(ret_decay, p3, p3, p3, p3, s0f, s0b)


def _merge_kernel(hy_ref, ret_ref, why_ref, wret_ref, ghy_ref, gret_ref, o_ref):
    a = _bdot(hy_ref[...], why_ref[...])
    b = _bdot(ret_ref[...], wret_ref[...])
    o_ref[...] = (ghy_ref[...].astype(F32) * a + gret_ref[...].astype(F32) * b).astype(BF16)


def _merge(hy, ret, p, w_hy, w_ret, d):
    m = hy.shape[0]
    tm = min(512, m)
    tn = min(1024, d)
    nb = d // tn
    return pl.pallas_call(
        _merge_kernel,
        grid=(m // tm, nb),
        in_specs=[
            pl.BlockSpec((tm, d), lambda i, j: (i, 0)),
            pl.BlockSpec((tm, 2 * d), lambda i, j: (i, 0)),
            pl.BlockSpec((d, tn), lambda i, j: (0, j)),
            pl.BlockSpec((2 * d, tn), lambda i, j: (0, j)),
            pl.BlockSpec((tm, tn), lambda i, j: (i, 10 * nb + j)),
            pl.BlockSpec((tm, tn), lambda i, j: (i, 11 * nb + j)),
        ],
        out_specs=pl.BlockSpec((tm, tn), lambda i, j: (i, j)),
        out_shape=jax.ShapeDtypeStruct((m, d), BF16),
        compiler_params=_cparams(("arbitrary", "arbitrary")),
        name="merge_proj",
    )(hy, ret, w_hy, w_ret, p, p)


def _wo_kernel(*refs, final):
    if final:
        m_ref, w_ref, x_ref, g_ref, fg_ref, o_ref = refs
    else:
        m_ref, w_ref, x_ref, g_ref, o_ref = refs
    xn = x_ref[0] + g_ref[0] * _bdot(m_ref[0], w_ref[...])
    if final:
        xn = xn * lax.rsqrt(jnp.mean(xn * xn, axis=-1, keepdims=True) + NORM_EPS) * fg_ref[...]
    o_ref[0] = xn


def _wo_residual(mrg, w_o, x, gate, final_g=None):
    b, l, d = x.shape
    tl = min(512, l)
    final = final_g is not None
    in_specs = [
        pl.BlockSpec((1, tl, d), lambda i, j: (i, j, 0)),
        pl.BlockSpec((d, d), lambda i, j: (0, 0)),
        pl.BlockSpec((1, tl, d), lambda i, j: (i, j, 0)),
        pl.BlockSpec((1, 1, d), lambda i, j: (i, 0, 0)),
    ]
    args = [mrg.reshape(b, l, d), w_o, x, gate]
    if final:
        in_specs.append(pl.BlockSpec((1, d), lambda i, j: (0, 0)))
        args.append(final_g.reshape(1, d))
    return pl.pallas_call(
        functools.partial(_wo_kernel, final=final),
        grid=(b, l // tl),
        in_specs=in_specs,
        out_specs=pl.BlockSpec((1, tl, d), lambda i, j: (i, j, 0)),
        out_shape=jax.ShapeDtypeStruct((b, l, d), F32),
        compiler_params=_cparams(("arbitrary", "arbitrary")),
        name="wo_residual",
    )(*args)


def _mixer(h, use_rope, s0f, s0b, w_in_b, conv_w, conv_b, kf, consts, hy_bias, ret_decay,
           w_hy_b, w_ret_b):
    b, l, d = h.shape
    p = _inproj(h.reshape(b * l, d), w_in_b, l, use_rope)
    p3 = p.reshape(b, l, -1)
    u, m = _hyena_front(p3, conv_w, conv_b, d)
    hy = _long_conv(u, m, hy_bias, kf, consts)
    ret, sf, sb = _retention(p3, ret_decay, s0f, s0b, d)
    mrg = _merge(hy.reshape(b * l, d), ret.reshape(b * l, 2 * d), p, w_hy_b, w_ret_b, d)
    return mrg, sf, sb


def kernel(x, c, ctx, c_ctx, ln_g, ada_w, ada_b, w_in, hy_conv_w, hy_conv_b, hy_filt_w1,
           hy_filt_b1, hy_filt_w2, hy_filt_b2, hy_filt_w3, hy_filt_b3, hy_filt_freq,
           hy_filt_wout, hy_bias, ret_decay, w_hy_out, w_ret_out, w_o, final_g):
    b, l, d = x.shape
    lc = ctx.shape[1]
    depth = ln_g.shape[0]
    h = d // QK_DIM
    assert b % 2 == 0 and d % QK_DIM == 0 and l % GRID_W == 0

    rows = -(-(b + 1) // 8) * 8
    cond = jnp.zeros((rows, d), F32).at[:b].set(c).at[b].set(c_ctx)
    mod = _ada_modulation(cond, ada_w, ada_b)

    consts_l = _dft_consts(l, FFT_N2 if (2 * l) % FFT_N2 == 0 and l >= 1024 else 1)
    consts_c = _dft_consts(lc, 1)
    zero_state = jnp.zeros((b, h, QK_DIM, V_DIM), F32)

    for i in range(depth):
        sh, sc, gt = mod[i, :, :d], mod[i, :, d:2 * d], mod[i, :, 2 * d:]
        lat = lambda a: a[:b, None, :]
        cx = lambda a: jnp.broadcast_to(a[b][None, None, :], (b, 1, d))
        w_in_b = w_in[i].astype(BF16)
        w_hy_b = w_hy_out[i].astype(BF16)
        w_ret_b = w_ret_out[i].astype(BF16)
        w_o_b = w_o[i].astype(BF16)
        filt = (hy_filt_w1[i], hy_filt_b1[i], hy_filt_w2[i], hy_filt_b2[i], hy_filt_w3[i],
                hy_filt_b3[i], hy_filt_freq[i], hy_filt_wout[i])
        params = lambda kf, consts: (w_in_b, hy_conv_w[i], hy_conv_b[i], kf, consts, hy_bias[i],
                                     ret_decay[i], w_hy_b, w_ret_b)

        h_ctx = _prenorm(ctx, ln_g[i], cx(sc), cx(sh))
        if i < depth - 1:
            kf_c = _filter_spectrum(_hyena_filter_taps(lc, d, *filt), consts_c)
            mrg_c, s_ctx_f, s_ctx_b = _mixer(h_ctx, False, zero_state, zero_state,
                                             *params(kf_c, consts_c))
            ctx_next = _wo_residual(mrg_c, w_o_b, ctx, cx(gt))
        else:
            p_c = _inproj(h_ctx.reshape(b * lc, d), w_in_b, lc, False)
            _, s_ctx_f, s_ctx_b = _retention(p_c.reshape(b, lc, -1), ret_decay[i],
                                             zero_state, zero_state, d)
            ctx_next = ctx

        h_lat = _prenorm(x, ln_g[i], lat(sc), lat(sh))
        kf_l = _filter_spectrum(_hyena_filter_taps(l, d, *filt), consts_l)
        mrg, _, _ = _mixer(h_lat, True, s_ctx_f, s_ctx_b, *params(kf_l, consts_l))
        x = _wo_residual(mrg, w_o_b, x, lat(gt), final_g if i == depth - 1 else None)
        ctx = ctx_next

    return x
```

```python
import functools
import math

import numpy as np
import jax
import jax.numpy as jnp
from jax import lax
from jax.experimental import pallas as pl
from jax.experimental.pallas import tpu as pltpu

F32 = jnp.float32
BF16 = jnp.bfloat16

NORM_EPS = 1e-6
GRID_W = 64
ROPE_BASE = 10000.0
QK_DIM = 256
V_DIM = 512
RET_CHUNK = 256
HY_EMB = 33
HY_BANDS = (HY_EMB - 1) // 2
HY_EMB_PAD = 64
HY_FAST_DECAY_PCT = 0.3
HY_SLOW_DECAY_PCT = 1.5
HY_DECAY_TARGET = 1e-2
FFT_N2 = 64
PROJ_NSUB = 4
VMEM_LIMIT = 56 * 1024 * 1024


def _cparams(sem):
    return pltpu.CompilerParams(dimension_semantics=sem, vmem_limit_bytes=VMEM_LIMIT)


def _silu(x):
    return x / (1.0 + jnp.exp(-x))


def _sigmoid(x):
    return 1.0 / (1.0 + jnp.exp(-x))


def _bdot(a, b):
    return jnp.dot(a, b, preferred_element_type=F32)


def _ada_kernel(c_ref, w_ref, b_ref, o_ref):
    s = _silu(c_ref[...]).astype(BF16)
    o_ref[0] = _bdot(s, w_ref[0].astype(BF16)) + b_ref[0]


def _ada_modulation(cond, ada_w, ada_b):
    depth, d, w3 = ada_w.shape
    r = cond.shape[0]
    tn = min(512, w3)
    return pl.pallas_call(
        _ada_kernel,
        grid=(depth, w3 // tn),
        in_specs=[
            pl.BlockSpec((r, d), lambda l, j: (0, 0)),
            pl.BlockSpec((1, d, tn), lambda l, j: (l, 0, j)),
            pl.BlockSpec((1, 1, tn), lambda l, j: (l, 0, j)),
        ],
        out_specs=pl.BlockSpec((1, r, tn), lambda l, j: (l, 0, j)),
        out_shape=jax.ShapeDtypeStruct((depth, r, w3), F32),
        compiler_params=_cparams(("arbitrary", "arbitrary")),
        name="ada_mod",
    )(cond, ada_w, ada_b.reshape(depth, 1, w3))


def _prenorm_kernel(x_ref, g_ref, sc_ref, sh_ref, o_ref):
    x = x_ref[0]
    y = x * lax.rsqrt(jnp.mean(x * x, axis=-1, keepdims=True) + NORM_EPS)
    o_ref[0] = ((y * g_ref[...]) * (1.0 + sc_ref[0]) + sh_ref[0]).astype(BF16)


def _prenorm(x, g, sc, sh):
    b, l, d = x.shape
    tl = min(512, l)
    return pl.pallas_call(
        _prenorm_kernel,
        grid=(b, l // tl),
        in_specs=[
            pl.BlockSpec((1, tl, d), lambda i, j: (i, j, 0)),
            pl.BlockSpec((1, d), lambda i, j: (0, 0)),
            pl.BlockSpec((1, 1, d), lambda i, j: (i, 0, 0)),
            pl.BlockSpec((1, 1, d), lambda i, j: (i, 0, 0)),
        ],
        out_specs=pl.BlockSpec((1, tl, d), lambda i, j: (i, j, 0)),
        out_shape=jax.ShapeDtypeStruct((b, l, d), BF16),
        compiler_params=_cparams(("arbitrary", "arbitrary")),
        name="prenorm",
    )(x, g.reshape(1, d), sc, sh)


def _proj_kernel(*refs, epi, nsub, tm, tn, r, hw, j0):
    if epi == "rope":
        h_ref, w_ref, cos_ref, sin_ref, o_ref, wb_scr = refs
    else:
        h_ref, w_ref, o_ref, wb_scr = refs

    @pl.when(pl.program_id(1) == 0)
    def _():
        wb_scr[...] = w_ref[...].astype(BF16)

    qk_scale = jnp.where(j0 + pl.program_id(0) < r, 1.0, QK_DIM ** -0.5).astype(F32)
    ts = tm // nsub
    sw = 128 if epi == "rope" else (hw or tn)
    for s in range(nsub):
        rows = slice(s * ts, (s + 1) * ts)
        acc = _bdot(h_ref[rows, :], wb_scr[...])
        for g in range(tn // sw):
            val = acc[:, g * sw:(g + 1) * sw]
            if epi == "rope":
                t = (g % 2) * 128
                val = (val * cos_ref[rows, t:t + 128]
                       + pltpu.roll(val, 64, 1) * sin_ref[rows, t:t + 128]) * qk_scale
            elif epi == "qkscale":
                val = val * qk_scale
            elif epi == "silu":
                val = _silu(val)
            elif epi == "sigmoid":
                val = _sigmoid(val)
            val = val.astype(BF16)
            if hw:
                c0 = g * sw
                o_ref[c0 // hw, rows, c0 % hw:c0 % hw + sw] = val
            else:
                o_ref[rows, g * sw:(g + 1) * sw] = val


def _rope_tables(l):
    quarter = QK_DIM // 4
    inv = 1.0 / (ROPE_BASE ** (jnp.arange(quarter, dtype=F32) / quarter))
    t = jnp.arange(l)
    row = (t // GRID_W).astype(F32)
    col = (t % GRID_W).astype(F32)
    ar = row[:, None] * inv[None, :]
    ac = col[:, None] * inv[None, :]
    cos_t = jnp.concatenate([jnp.cos(ar), jnp.cos(ar), jnp.cos(ac), jnp.cos(ac)], axis=-1)
    sin_t = jnp.concatenate([-jnp.sin(ar), jnp.sin(ar), -jnp.sin(ac), jnp.sin(ac)], axis=-1)
    return cos_t, sin_t


def _proj(h, w, col0, ncols, l, epi, hw=None):
    m, d = h.shape
    tm = min(1024, l)
    tn = min(1024, d)
    j0 = col0 // tn
    in_specs = [
        pl.BlockSpec((tm, d), lambda j, i: (i, 0)),
        pl.BlockSpec((d, tn), lambda j, i: (0, j0 + j)),
    ]
    args = [h, w]
    if epi == "rope":
        lb = l // tm
        in_specs += [pl.BlockSpec((tm, QK_DIM), lambda j, i: (i % lb, 0))] * 2
        args += list(_rope_tables(l))
    if hw:
        out_spec = pl.BlockSpec((tn // hw, tm, hw), lambda j, i: (j, i, 0))
        out_shape = jax.ShapeDtypeStruct((ncols // hw, m, hw), BF16)
    else:
        out_spec = pl.BlockSpec((tm, tn), lambda j, i: (i, j))
        out_shape = jax.ShapeDtypeStruct((m, ncols), BF16)
    return pl.pallas_call(
        functools.partial(_proj_kernel, epi=epi, nsub=PROJ_NSUB, tm=tm, tn=tn, r=d // tn, hw=hw,
                          j0=j0),
        grid=(ncols // tn, m // tm),
        in_specs=in_specs,
        out_specs=out_spec,
        out_shape=out_shape,
        scratch_shapes=[pltpu.VMEM((d, tn), BF16)],
        compiler_params=_cparams(("arbitrary", "arbitrary")),
        name="in_proj_" + epi,
    )(*args)


def _hyfront_kernel(v_ref, x0_ref, x1_ref, g_ref, wv_ref, w0_ref, w1_ref,
                    bv_ref, b0_ref, b1_ref, u_ref, m_ref, *, l):
    rows = lax.broadcasted_iota(jnp.int32, (l, 1), 0)

    def conv3(x_ref, w_ref, b_ref):
        x = x_ref[0].astype(F32)
        prev = jnp.where(rows == 0, 0.0, pltpu.roll(x, 1, 0))
        nxt = jnp.where(rows == l - 1, 0.0, pltpu.roll(x, l - 1, 0))
        return prev * w_ref[0:1, :] + x * w_ref[1:2, :] + nxt * w_ref[2:3, :] + b_ref[...]

    hv = conv3(v_ref, wv_ref, bv_ref)
    hx1 = conv3(x1_ref, w1_ref, b1_ref)
    u_ref[0] = (hv * hx1).astype(BF16)
    hx0 = conv3(x0_ref, w0_ref, b0_ref)
    m_ref[0] = (hx0 * g_ref[0].astype(F32)).astype(BF16)


def _hyena_front(phy, phg, conv_w, conv_b):
    b, l, d = phg.shape
    cs = 128
    nb = d // cs
    pspec = lambda off: pl.BlockSpec((1, l, cs), lambda i, j: (i, 0, off * nb + j))
    wspec = lambda off: pl.BlockSpec((3, cs), lambda i, j: (0, off * nb + j))
    bspec = lambda off: pl.BlockSpec((1, cs), lambda i, j: (0, off * nb + j))
    ospec = pl.BlockSpec((1, l, cs), lambda i, j: (i, 0, j))
    return pl.pallas_call(
        functools.partial(_hyfront_kernel, l=l),
        grid=(b, nb),
        in_specs=[pspec(0), pspec(1), pspec(2), pspec(0),
                  wspec(0), wspec(1), wspec(2), bspec(0), bspec(1), bspec(2)],
        out_specs=[ospec, ospec],
        out_shape=[jax.ShapeDtypeStruct((b, l, d), BF16)] * 2,
        compiler_params=_cparams(("arbitrary", "arbitrary")),
        name="hyena_front",
    )(phy, phy, phy, phg, conv_w, conv_w, conv_w,
      conv_b.reshape(1, -1), conv_b.reshape(1, -1), conv_b.reshape(1, -1))


def _filter_kernel(z_ref, t_ref, w1_ref, b1_ref, w2_ref, b2_ref, w3_ref, b3_ref, fr_ref,
                   wo_ref, dl_ref, o_ref, *, tr, l):
    dot = functools.partial(jnp.dot, precision=lax.Precision.HIGHEST, preferred_element_type=F32)
    f = fr_ref[...]
    h = jnp.sin(f * (dot(z_ref[...], w1_ref[...]) + b1_ref[...]))
    h = jnp.sin(f * (dot(h, w2_ref[...]) + b2_ref[...]))
    h = jnp.sin(f * (dot(h, w3_ref[...]) + b3_ref[...]))
    y = dot(h, wo_ref[...])
    win = jnp.exp(-t_ref[...] * dl_ref[...])
    rows = pl.program_id(0) * tr + lax.broadcasted_iota(jnp.int32, (tr, 1), 0)
    o_ref[...] = jnp.where(rows == l, 0.0, y * win).astype(BF16)


def _hyena_filter_taps(l, c, w1, b1, w2, b2, w3, b3, freq, w_out):
    t = jnp.linspace(0.0, 1.0, l, dtype=F32)[:, None]
    ang = 2.0 * math.pi * jnp.arange(l, dtype=F32)[:, None] / l
    f = jnp.linspace(1e-4, HY_BANDS - 1, HY_BANDS, dtype=F32)[None, :]
    z = jnp.concatenate([t, jnp.cos(f * ang), -jnp.sin(f * ang)], axis=-1)
    z = jnp.pad(z, ((0, 0), (0, HY_EMB_PAD - HY_EMB)))
    back = lambda a: jnp.concatenate([a[l - 1:l], a[:0:-1]], axis=0)
    z2 = jnp.concatenate([z, back(z)], axis=0)
    t2 = jnp.concatenate([t, back(t)], axis=0)
    max_decay = math.log(HY_DECAY_TARGET) / HY_FAST_DECAY_PCT
    min_decay = math.log(HY_DECAY_TARGET) / HY_SLOW_DECAY_PCT
    deltas = jnp.abs(jnp.linspace(min_decay, max_decay, c, dtype=F32))[None, :]
    w1p = jnp.pad(w1, ((0, HY_EMB_PAD - HY_EMB), (0, 0)))
    hid = w2.shape[0]
    tr = min(512, l)
    lb = l // tr
    full = lambda shape: pl.BlockSpec(shape, lambda i: (0, 0))
    return pl.pallas_call(
        functools.partial(_filter_kernel, tr=tr, l=l),
        grid=(2 * lb,),
        in_specs=[
            pl.BlockSpec((tr, HY_EMB_PAD), lambda i: (i, 0)),
            pl.BlockSpec((tr, 1), lambda i: (i, 0)),
            full((HY_EMB_PAD, hid)), full((1, hid)),
            full((hid, hid)), full((1, hid)),
            full((hid, hid)), full((1, hid)),
            full((1, hid)),
            pl.BlockSpec((hid, c), lambda i: (0, i // lb)),
            full((1, c)),
        ],
        out_specs=pl.BlockSpec((tr, c), lambda i: (i, 0)),
        out_shape=jax.ShapeDtypeStruct((2 * l, c), BF16),
        compiler_params=_cparams(("arbitrary",)),
        name="hyena_filter",
    )(z2, t2, w1p, b1.reshape(1, -1), w2, b2.reshape(1, -1), w3, b3.reshape(1, -1),
      freq.reshape(1, -1), w_out, deltas)


def _dft_consts(l, n2):
    n = 2 * l
    n1 = n // n2
    nin = n1 // 2

    def cs(k, m, period):
        ph = 2.0 * np.pi * ((np.outer(k, m)) % period) / period
        return np.cos(ph), np.sin(ph)

    k1 = np.arange(n1)
    c, s = cs(k1, np.arange(nin), n1)
    f1 = np.block([[c, s], [-s, c]])
    cf, sf = cs(k1, np.arange(n1), n1)
    f1_real = np.concatenate([cf, -sf], axis=0)
    ci, si = cs(np.arange(nin), k1, n1)
    g1 = np.block([[ci, -si], [si, ci]]) / n
    out = dict(n1=n1, n2=n2, nin=nin,
               f1=jnp.asarray(f1, F32).astype(BF16),
               f1_real=jnp.asarray(f1_real, F32).astype(BF16),
               g1=jnp.asarray(g1, F32).astype(BF16))
    if n2 > 1:
        m2 = np.arange(n2)
        ph = (m2[None, None, :] * k1[:, None, None] + n1 * m2[None, None, :] * m2[None, :, None]) % n
        ph = 2.0 * np.pi * ph / n
        tr, ti = np.cos(ph), -np.sin(ph)
        t_fwd = np.concatenate([np.concatenate([tr, -ti], axis=2),
                                np.concatenate([ti, tr], axis=2)], axis=1)
        trt, tit = tr.transpose(0, 2, 1), ti.transpose(0, 2, 1)
        t_inv = np.concatenate([np.concatenate([trt, tit], axis=2),
                                np.concatenate([-tit, trt], axis=2)], axis=1)
        out["t_fwd"] = jnp.asarray(t_fwd, F32).astype(BF16)
        out["t_inv"] = jnp.asarray(t_inv, F32).astype(BF16)
    return out


def _cmm_kernel(*refs, mode, chunk, nchunk):
    a_ref, x_ref, o_ref = refs[0], refs[1], refs[-1]
    a = a_ref[...]
    hm = a.shape[0] // 2
    for cc in range(nchunk):
        sl = slice(cc * chunk, (cc + 1) * chunk)
        acc = _bdot(a, x_ref[0, :, sl])
        if mode == "kfmul":
            kf_ref = refs[2]
            xr, xi = acc[:hm], acc[hm:]
            kr, ki = kf_ref[0, :, sl], kf_ref[1, :, sl]
            o_ref[0, :hm, sl] = (xr * kr - xi * ki).astype(o_ref.dtype)
            o_ref[0, hm:, sl] = (xr * ki + xi * kr).astype(o_ref.dtype)
        elif mode == "epi":
            u_ref, m_ref, b_ref = refs[2], refs[3], refs[4]
            u = u_ref[0, :, sl].astype(F32)
            o = (acc + b_ref[:, sl] * u) * m_ref[0, :, sl].astype(F32)
            o_ref[0, :, sl] = o.astype(o_ref.dtype)
        else:
            o_ref[0, :, sl] = acc.astype(o_ref.dtype)


def _cmm(a, x, mode="plain", extra=(), out_dtype=BF16):
    mr, k = a.shape
    g, _, cols = x.shape
    tc = min(8192, cols)
    chunk = min(1024, tc)
    in_specs = [pl.BlockSpec((mr, k), lambda i, j: (0, 0)),
                pl.BlockSpec((1, k, tc), lambda i, j: (i, 0, j))]
    if mode == "kfmul":
        in_specs += [pl.BlockSpec((2, mr // 2, tc), lambda i, j: (0, 0, j))]
    elif mode == "epi":
        in_specs += [pl.BlockSpec((1, mr, tc), lambda i, j: (i, 0, j)),
                     pl.BlockSpec((1, mr, tc), lambda i, j: (i, 0, j)),
                     pl.BlockSpec((1, tc), lambda i, j: (0, j))]
    return pl.pallas_call(
        functools.partial(_cmm_kernel, mode=mode, chunk=chunk, nchunk=tc // chunk),
        grid=(g, cols // tc),
        in_specs=in_specs,
        out_specs=pl.BlockSpec((1, mr, tc), lambda i, j: (i, 0, j)),
        out_shape=jax.ShapeDtypeStruct((g, mr, cols), out_dtype),
        compiler_params=_cparams(("arbitrary", "arbitrary")),
        name="dft_" + mode,
    )(a, x, *extra)


def _mid_kernel(a_ref, t_ref, kf_ref, ti_ref, o_ref, *, kb, n2):
    for kk in range(kb):
        a = jnp.concatenate([a_ref[0, 0, kk], a_ref[0, 1, kk]], axis=0)
        x = _bdot(t_ref[kk], a)
        xr, xi = x[:n2], x[n2:]
        kr, ki = kf_ref[0, kk], kf_ref[1, kk]
        y = jnp.concatenate([xr * kr - xi * ki, xr * ki + xi * kr], axis=0).astype(BF16)
        bm = _bdot(ti_ref[kk], y)
        o_ref[0, 0, kk] = bm[:n2].astype(BF16)
        o_ref[0, 1, kk] = bm[n2:].astype(BF16)


def _mid_filter_kernel(a_ref, t_ref, o_ref, *, kb, n2):
    for kk in range(kb):
        a = jnp.concatenate([a_ref[0, 0, kk], a_ref[0, 1, kk]], axis=0)
        x = _bdot(t_ref[kk], a)
        o_ref[0, kk] = x[:n2]
        o_ref[1, kk] = x[n2:]


def _mid_tiles(n1, c):
    return min(8, n1), min(512, c)


def _fft_mid(a5, kf, consts):
    p, _, n1, n2, c = a5.shape
    kb, cs = _mid_tiles(n1, c)
    aspec = pl.BlockSpec((1, 2, kb, n2, cs), lambda i, j, q: (q, 0, i, 0, j))
    tspec = pl.BlockSpec((kb, 2 * n2, 2 * n2), lambda i, j, q: (i, 0, 0))
    return pl.pallas_call(
        functools.partial(_mid_kernel, kb=kb, n2=n2),
        grid=(n1 // kb, c // cs, p),
        in_specs=[aspec, tspec,
                  pl.BlockSpec((2, kb, n2, cs), lambda i, j, q: (0, i, 0, j)),
                  tspec],
        out_specs=aspec,
        out_shape=jax.ShapeDtypeStruct(a5.shape, BF16),
        compiler_params=_cparams(("arbitrary", "arbitrary", "arbitrary")),
        name="dft_mid",
    )(a5, consts["t_fwd"], kf, consts["t_inv"])


def _fft_mid_filter(a5, consts):
    _, _, n1, n2, c = a5.shape
    kb, cs = _mid_tiles(n1, c)
    return pl.pallas_call(
        functools.partial(_mid_filter_kernel, kb=kb, n2=n2),
        grid=(n1 // kb, c // cs),
        in_specs=[pl.BlockSpec((1, 2, kb, n2, cs), lambda i, j: (0, 0, i, 0, j)),
                  pl.BlockSpec((kb, 2 * n2, 2 * n2), lambda i, j: (i, 0, 0))],
        out_specs=pl.BlockSpec((2, kb, n2, cs), lambda i, j: (0, i, 0, j)),
        out_shape=jax.ShapeDtypeStruct((2, n1, n2, c), F32),
        compiler_params=_cparams(("arbitrary", "arbitrary")),
        name="dft_mid_filter",
    )(a5, consts["t_fwd"])


def _filter_spectrum(taps, consts):
    n, c = taps.shape
    n1, n2 = consts["n1"], consts["n2"]
    x = taps.reshape(1, n1, n2 * c)
    if n2 == 1:
        return _cmm(consts["f1_real"], x, out_dtype=F32).reshape(2, n1, c)
    a = _cmm(consts["f1_real"], x)
    return _fft_mid_filter(a.reshape(1, 2, n1, n2, c), consts)


def _long_conv(u, m, bias, kf, consts):
    b, l, c = u.shape
    n1, n2, nin = consts["n1"], consts["n2"], consts["nin"]
    p = b // 2
    cols = n2 * c
    x = u.reshape(p, 2 * nin, cols)
    if n2 == 1:
        y = _cmm(consts["f1"], x, mode="kfmul", extra=(kf,))
    else:
        a = _cmm(consts["f1"], x)
        y = _fft_mid(a.reshape(p, 2, n1, n2, c), kf, consts).reshape(p, 2 * n1, cols)
    bias_t = jnp.tile(bias.reshape(1, c), (1, n2))
    out = _cmm(consts["g1"], y, mode="epi", extra=(x, m.reshape(p, 2 * nin, cols), bias_t))
    return out.reshape(b, l, c)


def _ret_kernel(*refs, nch, cl, states_only):
    if states_only:
        rd_ref, k_ref, v_ref, s0f_ref, s0b_ref, sf_ref, sb_ref, s_scr = refs
    else:
        (rd_ref, q_ref, k_ref, v_ref, g_ref, s0f_ref, s0b_ref,
         o_ref, sf_ref, sb_ref, s_scr, sball_scr, dmask_scr) = refs
    hd = pl.program_id(1)
    one = jnp.ones((1, 1), F32)
    lgf = -jnp.exp(one * rd_ref[0, hd])
    lgb = -jnp.exp(one * rd_ref[1, hd])
    ri = lax.broadcasted_iota(jnp.int32, (cl, 1), 0).astype(F32)
    ci = lax.broadcasted_iota(jnp.int32, (1, cl), 1).astype(F32)
    k_dec_f = jnp.exp(lgf * (cl - 1.0 - ci))
    k_dec_b = jnp.exp(lgb * ci)
    chunk_dec_f = jnp.exp(lgf * cl)
    chunk_dec_b = jnp.exp(lgb * cl)

    def k_transposed(k, dec):
        return (k.astype(F32).T * dec).astype(BF16)

    def load(ref, c):
        return ref[0, 0, pl.ds(pl.multiple_of(c * cl, cl), cl), :]

    s_scr[...] = s0b_ref[0, 0]

    def bstep(t, carry):
        c = nch - 1 - t
        if not states_only:
            sball_scr[c] = s_scr[...].astype(BF16)
        s_scr[...] = (s_scr[...] * chunk_dec_b
                      + _bdot(k_transposed(load(k_ref, c), k_dec_b), load(v_ref, c)))
        return carry

    lax.fori_loop(0, nch, bstep, 0)
    sb_ref[0, 0] = s_scr[...]

    s_scr[...] = s0f_ref[0, 0]
    if not states_only:
        q_dec_f = jnp.exp(lgf * (ri + 1.0))
        q_dec_b = jnp.exp(lgb * (cl - ri))
        diff = ri - ci
        dmask_scr[...] = (jnp.where(diff >= 0, jnp.exp(lgf * jnp.maximum(diff, 0.0)), 0.0)
                          + jnp.where(diff <= 0, jnp.exp(lgb * jnp.maximum(-diff, 0.0)), 0.0))

    def fstep(c, carry):
        k = load(k_ref, c)
        v = load(v_ref, c)
        if not states_only:
            q = load(q_ref, c)
            scores = lax.dot_general(q, k, (((1,), (1,)), ((), ())), preferred_element_type=F32)
            o = _bdot((scores * dmask_scr[...]).astype(BF16), v)
            qf = q.astype(F32)
            q2 = jnp.concatenate([(qf * q_dec_f).astype(BF16), (qf * q_dec_b).astype(BF16)], axis=1)
            s2 = jnp.concatenate([s_scr[...].astype(BF16), sball_scr[c]], axis=0)
            o = o + _bdot(q2, s2)
            o = o * lax.rsqrt(jnp.mean(o * o, axis=-1, keepdims=True) + NORM_EPS)
            r0 = pl.multiple_of(c * cl, cl)
            o_ref[0, pl.ds(r0, cl), :] = (o * load(g_ref, c).astype(F32)).astype(BF16)
        s_scr[...] = s_scr[...] * chunk_dec_f + _bdot(k_transposed(k, k_dec_f), v)
        return carry

    lax.fori_loop(0, nch, fstep, 0)
    sf_ref[0, 0] = s_scr[...]


def _retention(qk, k_off, v, gate, ret_decay, s0f, s0b, b, states_only=False):
    h = v.shape[0]
    l = v.shape[1] // b
    cl = min(RET_CHUNK, l)
    nch = l // cl
    sspec = pl.BlockSpec((1, 1, QK_DIM, V_DIM), lambda i, j: (i, j, 0, 0))
    state_shape = jax.ShapeDtypeStruct((b, h, QK_DIM, V_DIM), F32)
    hspec = lambda w, off: pl.BlockSpec((1, 1, l, w), lambda i, j: (off + j, i, 0, 0))
    qk4 = qk.reshape(k_off + h, b, l, QK_DIM)
    v4 = v.reshape(h, b, l, V_DIM)
    smem = pl.BlockSpec(memory_space=pltpu.SMEM)
    kern = functools.partial(_ret_kernel, nch=nch, cl=cl, states_only=states_only)
    if states_only:
        return pl.pallas_call(
            kern, grid=(b, h),
            in_specs=[smem, hspec(QK_DIM, k_off), hspec(V_DIM, 0), sspec, sspec],
            out_specs=[sspec, sspec],
            out_shape=[state_shape, state_shape],
            scratch_shapes=[pltpu.VMEM((QK_DIM, V_DIM), F32)],
            compiler_params=_cparams(("arbitrary", "arbitrary")),
            name="retention_states",
        )(ret_decay, qk4, v4, s0f, s0b)
    return pl.pallas_call(
        kern, grid=(b, h),
        in_specs=[smem, hspec(QK_DIM, 0), hspec(QK_DIM, k_off), hspec(V_DIM, 0), hspec(V_DIM, 0),
                  sspec, sspec],
        out_specs=[pl.BlockSpec((1, l, V_DIM), lambda i, j: (i, 0, j)), sspec, sspec],
        out_shape=[jax.ShapeDtypeStruct((b, l, h * V_DIM), BF16), state_shape, state_shape],
        scratch_shapes=[pltpu.VMEM((QK_DIM, V_DIM), F32),
                        pltpu.VMEM((nch, QK_DIM, V_DIM), BF16),
                        pltpu.VMEM((cl, cl), F32)],
        compiler_params=_cparams(("arbitrary", "arbitrary")),
        name="retention",
    )(ret_decay, qk4, qk4, v4, gate.reshape(h, b, l, V_DIM), s0f, s0b)


def _merge_kernel(hy_ref, ret_ref, why_ref, wret_ref, ghy_ref, gret_ref, o_ref):
    a = _bdot(hy_ref[...], why_ref[...])
    b = _bdot(ret_ref[...], wret_ref[...])
    o_ref[...] = (ghy_ref[...].astype(F32) * a + gret_ref[...].astype(F32) * b).astype(BF16)


def _merge(hy, ret, pmg, w_hy, w_ret):
    m, d = hy.shape
    tm = min(512, m)
    tn = min(1024, d)
    nb = d // tn
    return pl.pallas_call(
        _merge_kernel,
        grid=(m // tm, nb),
        in_specs=[
            pl.BlockSpec((tm, d), lambda i, j: (i, 0)),
            pl.BlockSpec((tm, 2 * d), lambda i, j: (i, 0)),
            pl.BlockSpec((d, tn), lambda i, j: (0, j)),
            pl.BlockSpec((2 * d, tn), lambda i, j: (0, j)),
            pl.BlockSpec((tm, tn), lambda i, j: (i, j)),
            pl.BlockSpec((tm, tn), lambda i, j: (i, nb + j)),
        ],
        out_specs=pl.BlockSpec((tm, tn), lambda i, j: (i, j)),
        out_shape=jax.ShapeDtypeStruct((m, d), BF16),
        compiler_params=_cparams(("arbitrary", "arbitrary")),
        name="merge_proj",
    )(hy, ret, w_hy, w_ret, pmg, pmg)


def _wo_kernel(*refs, final):
    if final:
        m_ref, w_ref, x_ref, g_ref, fg_ref, o_ref = refs
    else:
        m_ref, w_ref, x_ref, g_ref, o_ref = refs
    xn = x_ref[0] + g_ref[0] * _bdot(m_ref[0], w_ref[...])
    if final:
        xn = xn * lax.rsqrt(jnp.mean(xn * xn, axis=-1, keepdims=True) + NORM_EPS) * fg_ref[...]
    o_ref[0] = xn


def _wo_residual(mrg, w_o, x, gate, final_g=None):
    b, l, d = x.shape
    tl = min(512, l)
    final = final_g is not None
    in_specs = [
        pl.BlockSpec((1, tl, d), lambda i, j: (i, j, 0)),
        pl.BlockSpec((d, d), lambda i, j: (0, 0)),
        pl.BlockSpec((1, tl, d), lambda i, j: (i, j, 0)),
        pl.BlockSpec((1, 1, d), lambda i, j: (i, 0, 0)),
    ]
    args = [mrg.reshape(b, l, d), w_o, x, gate]
    if final:
        in_specs.append(pl.BlockSpec((1, d), lambda i, j: (0, 0)))
        args.append(final_g.reshape(1, d))
    return pl.pallas_call(
        functools.partial(_wo_kernel, final=final),
        grid=(b, l // tl),
        in_specs=in_specs,
        out_specs=pl.BlockSpec((1, tl, d), lambda i, j: (i, j, 0)),
        out_shape=jax.ShapeDtypeStruct((b, l, d), F32),
        compiler_params=_cparams(("arbitrary", "arbitrary")),
        name="wo_residual",
    )(*args)


def _mixer(h, use_rope, s0f, s0b, w_in, conv_w, conv_b, kf, consts, hy_bias, ret_decay,
           w_hy_b, w_ret_b):
    b, l, d = h.shape
    h2 = h.reshape(b * l, d)
    qk = _proj(h2, w_in, 0, 2 * d, l, "rope" if use_rope else "qkscale", hw=QK_DIM)
    v = _proj(h2, w_in, 2 * d, 2 * d, l, "plain", hw=V_DIM)
    rg = _proj(h2, w_in, 4 * d, 2 * d, l, "silu", hw=V_DIM)
    phy = _proj(h2, w_in, 6 * d, 3 * d, l, "plain")
    phg = _proj(h2, w_in, 9 * d, d, l, "silu")
    pmg = _proj(h2, w_in, 10 * d, 2 * d, l, "sigmoid")
    u, m = _hyena_front(phy.reshape(b, l, 3 * d), phg.reshape(b, l, d), conv_w, conv_b)
    hy = _long_conv(u, m, hy_bias, kf, consts)
    ret, sf, sb = _retention(qk, d // QK_DIM, v, rg, ret_decay, s0f, s0b, b)
    mrg = _merge(hy.reshape(b * l, d), ret.reshape(b * l, 2 * d), pmg, w_hy_b, w_ret_b)
    return mrg, sf, sb


def kernel(x, c, ctx, c_ctx, ln_g, ada_w, ada_b, w_in, hy_conv_w, hy_conv_b, hy_filt_w1,
           hy_filt_b1, hy_filt_w2, hy_filt_b2, hy_filt_w3, hy_filt_b3, hy_filt_freq,
           hy_filt_wout, hy_bias, ret_decay, w_hy_out, w_ret_out, w_o, final_g):
    b, l, d = x.shape
    lc = ctx.shape[1]
    depth = ln_g.shape[0]
    h = d // QK_DIM
    assert b % 2 == 0 and d % QK_DIM == 0 and l % GRID_W == 0

    rows = -(-(b + 1) // 8) * 8
    cond = jnp.zeros((rows, d), F32).at[:b].set(c).at[b].set(c_ctx)
    mod = _ada_modulation(cond, ada_w, ada_b)

    consts_l = _dft_consts(l, FFT_N2 if (2 * l) % FFT_N2 == 0 and l >= 1024 else 1)
    consts_c = _dft_consts(lc, 1)
    zero_state = jnp.zeros((b, h, QK_DIM, V_DIM), F32)

    for i in range(depth):
        sh, sc, gt = mod[i, :, :d], mod[i, :, d:2 * d], mod[i, :, 2 * d:]
        lat = lambda a: a[:b, None, :]
        cx = lambda a: jnp.broadcast_to(a[b][None, None, :], (b, 1, d))
        w_hy_b = w_hy_out[i].astype(BF16)
        w_ret_b = w_ret_out[i].astype(BF16)
        w_o_b = w_o[i].astype(BF16)
        filt = (hy_filt_w1[i], hy_filt_b1[i], hy_filt_w2[i], hy_filt_b2[i], hy_filt_w3[i],
                hy_filt_b3[i], hy_filt_freq[i], hy_filt_wout[i])
        params = lambda kf, consts: (w_in[i], hy_conv_w[i], hy_conv_b[i], kf, consts, hy_bias[i],
                                     ret_decay[i], w_hy_b, w_ret_b)

        h_ctx = _prenorm(ctx, ln_g[i], cx(sc), cx(sh))
        if i < depth - 1:
            kf_c = _filter_spectrum(_hyena_filter_taps(lc, d, *filt), consts_c)
            mrg_c, s_ctx_f, s_ctx_b = _mixer(h_ctx, False, zero_state, zero_state,
                                             *params(kf_c, consts_c))
            ctx_next = _wo_residual(mrg_c, w_o_b, ctx, cx(gt))
        else:
            hc2 = h_ctx.reshape(b * lc, d)
            k_c = _proj(hc2, w_in[i], d, d, lc, "qkscale", hw=QK_DIM)
            v_c = _proj(hc2, w_in[i], 2 * d, 2 * d, lc, "plain", hw=V_DIM)
            s_ctx_f, s_ctx_b = _retention(k_c, 0, v_c, None, ret_decay[i], zero_state, zero_state,
                                          b, states_only=True)
            ctx_next = ctx

        h_lat = _prenorm(x, ln_g[i], lat(sc), lat(sh))
        kf_l = _filter_spectrum(_hyena_filter_taps(l, d, *filt), consts_l)
        mrg, _, _ = _mixer(h_lat, True, s_ctx_f, s_ctx_b, *params(kf_l, consts_l))
        x = _wo_residual(mrg, w_o_b, x, lat(gt), final_g if i == depth - 1 else None)
        ctx = ctx_next

    return x
```

```python
import functools
import math

import numpy as np
import jax
import jax.numpy as jnp
from jax import lax
from jax.experimental import pallas as pl
from jax.experimental.pallas import tpu as pltpu

F32 = jnp.float32
BF16 = jnp.bfloat16

NORM_EPS = 1e-6
GRID_W = 64
ROPE_BASE = 10000.0
QK_DIM = 256
V_DIM = 512
RET_CHUNK = 256
HY_EMB = 33
HY_BANDS = (HY_EMB - 1) // 2
HY_EMB_PAD = 64
HY_FAST_DECAY_PCT = 0.3
HY_SLOW_DECAY_PCT = 1.5
HY_DECAY_TARGET = 1e-2
FFT_N2 = 64
PROJ_NSUB = 4
LANES = 128
FFT_PITCH = 72
FFT_UNROLL = 16
VMEM_LIMIT = 56 * 1024 * 1024


def _cparams(sem):
    return pltpu.CompilerParams(dimension_semantics=sem, vmem_limit_bytes=VMEM_LIMIT)


def _silu(x):
    return x / (1.0 + jnp.exp(-x))


def _sigmoid(x):
    return 1.0 / (1.0 + jnp.exp(-x))


def _bdot(a, b):
    return jnp.dot(a, b, preferred_element_type=F32)


def _ada_kernel(c_ref, w_ref, b_ref, o_ref):
    s = _silu(c_ref[...]).astype(BF16)
    o_ref[0] = _bdot(s, w_ref[0].astype(BF16)) + b_ref[0]


def _ada_modulation(cond, ada_w, ada_b):
    depth, d, w3 = ada_w.shape
    r = cond.shape[0]
    tn = min(512, w3)
    return pl.pallas_call(
        _ada_kernel,
        grid=(depth, w3 // tn),
        in_specs=[
            pl.BlockSpec((r, d), lambda l, j: (0, 0)),
            pl.BlockSpec((1, d, tn), lambda l, j: (l, 0, j)),
            pl.BlockSpec((1, 1, tn), lambda l, j: (l, 0, j)),
        ],
        out_specs=pl.BlockSpec((1, r, tn), lambda l, j: (l, 0, j)),
        out_shape=jax.ShapeDtypeStruct((depth, r, w3), F32),
        compiler_params=_cparams(("arbitrary", "arbitrary")),
        name="ada_mod",
    )(cond, ada_w, ada_b.reshape(depth, 1, w3))


def _prenorm_kernel(x_ref, g_ref, sc_ref, sh_ref, o_ref):
    x = x_ref[0]
    y = x * lax.rsqrt(jnp.mean(x * x, axis=-1, keepdims=True) + NORM_EPS)
    o_ref[0] = ((y * g_ref[...]) * (1.0 + sc_ref[0]) + sh_ref[0]).astype(BF16)


def _prenorm(x, g, sc, sh):
    b, l, d = x.shape
    tl = min(512, l)
    return pl.pallas_call(
        _prenorm_kernel,
        grid=(b, l // tl),
        in_specs=[
            pl.BlockSpec((1, tl, d), lambda i, j: (i, j, 0)),
            pl.BlockSpec((1, d), lambda i, j: (0, 0)),
            pl.BlockSpec((1, 1, d), lambda i, j: (i, 0, 0)),
            pl.BlockSpec((1, 1, d), lambda i, j: (i, 0, 0)),
        ],
        out_specs=pl.BlockSpec((1, tl, d), lambda i, j: (i, j, 0)),
        out_shape=jax.ShapeDtypeStruct((b, l, d), BF16),
        compiler_params=_cparams(("arbitrary", "arbitrary")),
        name="prenorm",
    )(x, g.reshape(1, d), sc, sh)


def _proj_kernel(*refs, epi, nsub, tm, tn, r, hw, j0):
    if epi == "rope":
        h_ref, w_ref, cos_ref, sin_ref, o_ref, wb_scr = refs
    else:
        h_ref, w_ref, o_ref, wb_scr = refs

    @pl.when(pl.program_id(1) == 0)
    def _():
        wb_scr[...] = w_ref[...].astype(BF16)

    qk_scale = jnp.where(j0 + pl.program_id(0) < r, 1.0, QK_DIM ** -0.5).astype(F32)
    ts = tm // nsub
    sw = 128 if epi == "rope" else (hw or tn)
    for s in range(nsub):
        rows = slice(s * ts, (s + 1) * ts)
        acc = _bdot(h_ref[rows, :], wb_scr[...])
        for g in range(tn // sw):
            val = acc[:, g * sw:(g + 1) * sw]
            if epi == "rope":
                t = (g % 2) * 128
                val = (val * cos_ref[rows, t:t + 128]
                       + pltpu.roll(val, 64, 1) * sin_ref[rows, t:t + 128]) * qk_scale
            elif epi == "qkscale":
                val = val * qk_scale
            elif epi == "silu":
                val = _silu(val)
            elif epi == "sigmoid":
                val = _sigmoid(val)
            val = val.astype(BF16)
            if hw:
                c0 = g * sw
                o_ref[c0 // hw, rows, c0 % hw:c0 % hw + sw] = val
            else:
                o_ref[rows, g * sw:(g + 1) * sw] = val


def _rope_tables(l):
    quarter = QK_DIM // 4
    inv = 1.0 / (ROPE_BASE ** (jnp.arange(quarter, dtype=F32) / quarter))
    t = jnp.arange(l)
    row = (t // GRID_W).astype(F32)
    col = (t % GRID_W).astype(F32)
    ar = row[:, None] * inv[None, :]
    ac = col[:, None] * inv[None, :]
    cos_t = jnp.concatenate([jnp.cos(ar), jnp.cos(ar), jnp.cos(ac), jnp.cos(ac)], axis=-1)
    sin_t = jnp.concatenate([-jnp.sin(ar), jnp.sin(ar), -jnp.sin(ac), jnp.sin(ac)], axis=-1)
    return cos_t, sin_t


def _proj(h, w, layer, col0, ncols, l, epi, hw=None):
    m, d = h.shape
    tm = min(1024, l)
    tn = min(1024, d)
    j0 = col0 // tn
    in_specs = [
        pl.BlockSpec((tm, d), lambda j, i: (i, 0)),
        pl.BlockSpec((None, d, tn), lambda j, i: (layer, 0, j0 + j)),
    ]
    args = [h, w]
    if epi == "rope":
        lb = l // tm
        in_specs += [pl.BlockSpec((tm, QK_DIM), lambda j, i: (i % lb, 0))] * 2
        args += list(_rope_tables(l))
    if hw:
        out_spec = pl.BlockSpec((tn // hw, tm, hw), lambda j, i: (j, i, 0))
        out_shape = jax.ShapeDtypeStruct((ncols // hw, m, hw), BF16)
    else:
        out_spec = pl.BlockSpec((tm, tn), lambda j, i: (i, j))
        out_shape = jax.ShapeDtypeStruct((m, ncols), BF16)
    return pl.pallas_call(
        functools.partial(_proj_kernel, epi=epi, nsub=PROJ_NSUB, tm=tm, tn=tn, r=d // tn, hw=hw,
                          j0=j0),
        grid=(ncols // tn, m // tm),
        in_specs=in_specs,
        out_specs=out_spec,
        out_shape=out_shape,
        scratch_shapes=[pltpu.VMEM((d, tn), BF16)],
        compiler_params=_cparams(("arbitrary", "arbitrary")),
        name="in_proj_" + epi,
    )(*args)


def _hyfront_kernel(v_ref, x0_ref, x1_ref, g_ref, wv_ref, w0_ref, w1_ref,
                    bv_ref, b0_ref, b1_ref, u_ref, m_ref, *, l):
    rows = lax.broadcasted_iota(jnp.int32, (l, 1), 0)

    def conv3(x_ref, w_ref, b_ref):
        x = x_ref[0].astype(F32)
        prev = jnp.where(rows == 0, 0.0, pltpu.roll(x, 1, 0))
        nxt = jnp.where(rows == l - 1, 0.0, pltpu.roll(x, l - 1, 0))
        return prev * w_ref[0:1, :] + x * w_ref[1:2, :] + nxt * w_ref[2:3, :] + b_ref[...]

    hv = conv3(v_ref, wv_ref, bv_ref)
    hx1 = conv3(x1_ref, w1_ref, b1_ref)
    u_ref[0] = (hv * hx1).astype(BF16)
    hx0 = conv3(x0_ref, w0_ref, b0_ref)
    m_ref[0] = (hx0 * g_ref[0].astype(F32)).astype(BF16)


def _hyena_front(phy, phg, conv_w, conv_b):
    b, l, d = phg.shape
    cs = 128
    nb = d // cs
    pspec = lambda off: pl.BlockSpec((1, l, cs), lambda i, j: (i, 0, off * nb + j))
    wspec = lambda off: pl.BlockSpec((3, cs), lambda i, j: (0, off * nb + j))
    bspec = lambda off: pl.BlockSpec((1, cs), lambda i, j: (0, off * nb + j))
    ospec = pl.BlockSpec((1, l, cs), lambda i, j: (i, 0, j))
    return pl.pallas_call(
        functools.partial(_hyfront_kernel, l=l),
        grid=(b, nb),
        in_specs=[pspec(0), pspec(1), pspec(2), pspec(0),
                  wspec(0), wspec(1), wspec(2), bspec(0), bspec(1), bspec(2)],
        out_specs=[ospec, ospec],
        out_shape=[jax.ShapeDtypeStruct((b, l, d), BF16)] * 2,
        compiler_params=_cparams(("arbitrary", "arbitrary")),
        name="hyena_front",
    )(phy, phy, phy, phg, conv_w, conv_w, conv_w,
      conv_b.reshape(1, -1), conv_b.reshape(1, -1), conv_b.reshape(1, -1))


def _filter_kernel(z_ref, t_ref, w1_ref, b1_ref, w2_ref, b2_ref, w3_ref, b3_ref, fr_ref,
                   wo_ref, dl_ref, o_ref, *, tr, l):
    dot = functools.partial(jnp.dot, precision=lax.Precision.HIGHEST, preferred_element_type=F32)
    f = fr_ref[...]
    h = jnp.sin(f * (dot(z_ref[...], w1_ref[...]) + b1_ref[...]))
    h = jnp.sin(f * (dot(h, w2_ref[...]) + b2_ref[...]))
    h = jnp.sin(f * (dot(h, w3_ref[...]) + b3_ref[...]))
    y = dot(h, wo_ref[...])
    win = jnp.exp(-t_ref[...] * dl_ref[...])
    rows = pl.program_id(0) * tr + lax.broadcasted_iota(jnp.int32, (tr, 1), 0)
    o_ref[...] = jnp.where(rows == l, 0.0, y * win).astype(BF16)


def _hyena_filter_taps(l, c, w1, b1, w2, b2, w3, b3, freq, w_out):
    t = jnp.linspace(0.0, 1.0, l, dtype=F32)[:, None]
    ang = 2.0 * math.pi * jnp.arange(l, dtype=F32)[:, None] / l
    f = jnp.linspace(1e-4, HY_BANDS - 1, HY_BANDS, dtype=F32)[None, :]
    z = jnp.concatenate([t, jnp.cos(f * ang), -jnp.sin(f * ang)], axis=-1)
    z = jnp.pad(z, ((0, 0), (0, HY_EMB_PAD - HY_EMB)))
    back = lambda a: jnp.concatenate([a[l - 1:l], a[:0:-1]], axis=0)
    z2 = jnp.concatenate([z, back(z)], axis=0)
    t2 = jnp.concatenate([t, back(t)], axis=0)
    max_decay = math.log(HY_DECAY_TARGET) / HY_FAST_DECAY_PCT
    min_decay = math.log(HY_DECAY_TARGET) / HY_SLOW_DECAY_PCT
    deltas = jnp.abs(jnp.linspace(min_decay, max_decay, c, dtype=F32))[None, :]
    w1p = jnp.pad(w1, ((0, HY_EMB_PAD - HY_EMB), (0, 0)))
    hid = w2.shape[0]
    tr = min(512, l)
    lb = l // tr
    full = lambda shape: pl.BlockSpec(shape, lambda i: (0, 0))
    return pl.pallas_call(
        functools.partial(_filter_kernel, tr=tr, l=l),
        grid=(2 * lb,),
        in_specs=[
            pl.BlockSpec((tr, HY_EMB_PAD), lambda i: (i, 0)),
            pl.BlockSpec((tr, 1), lambda i: (i, 0)),
            full((HY_EMB_PAD, hid)), full((1, hid)),
            full((hid, hid)), full((1, hid)),
            full((hid, hid)), full((1, hid)),
            full((1, hid)),
            pl.BlockSpec((hid, c), lambda i: (0, i // lb)),
            full((1, c)),
        ],
        out_specs=pl.BlockSpec((tr, c), lambda i: (i, 0)),
        out_shape=jax.ShapeDtypeStruct((2 * l, c), BF16),
        compiler_params=_cparams(("arbitrary",)),
        name="hyena_filter",
    )(z2, t2, w1p, b1.reshape(1, -1), w2, b2.reshape(1, -1), w3, b3.reshape(1, -1),
      freq.reshape(1, -1), w_out, deltas)


def _dft_consts(l, n2):
    n = 2 * l
    n1 = n // n2
    nin = n1 // 2

    def cs(k, m, period):
        ph = 2.0 * np.pi * ((np.outer(k, m)) % period) / period
        return np.cos(ph), np.sin(ph)

    k1 = np.arange(n1)
    c, s = cs(k1, np.arange(nin), n1)
    f1 = np.block([[c, s], [-s, c]])
    cf, sf = cs(k1, np.arange(n1), n1)
    f1_real = np.concatenate([cf, -sf], axis=0)
    ci, si = cs(np.arange(nin), k1, n1)
    g1 = np.block([[ci, -si], [si, ci]]) / n
    out = dict(n1=n1, n2=n2, nin=nin,
               f1=jnp.asarray(f1, F32).astype(BF16),
               f1_real=jnp.asarray(f1_real, F32).astype(BF16),
               g1=jnp.asarray(g1, F32).astype(BF16))
    if n2 > 1:
        f1_il = np.empty_like(f1)
        f1_il[0::2], f1_il[1::2] = f1[:n1], f1[n1:]
        out["f1_il"] = jnp.asarray(f1_il, F32).astype(BF16)
        out["g1_il"] = jnp.asarray(f1_il.T / n, F32).astype(BF16)
        m2 = np.arange(n2)
        ph = (m2[None, None, :] * k1[:, None, None] + n1 * m2[None, None, :] * m2[None, :, None]) % n
        ph = 2.0 * np.pi * ph / n
        tr, ti = np.cos(ph), -np.sin(ph)
        t_fwd = np.concatenate([np.concatenate([tr, -ti], axis=2),
                                np.concatenate([ti, tr], axis=2)], axis=1)
        out["t_fwd"] = jnp.asarray(t_fwd, F32).astype(BF16)
        t_il = np.empty_like(t_fwd)
        t_il[:, :, 0::2], t_il[:, :, 1::2] = t_fwd[:, :, :n2], t_fwd[:, :, n2:]
        out["t_il"] = jnp.asarray(t_il, F32).astype(BF16)
    return out


def _cmm_kernel(*refs, mode, chunk, nchunk):
    a_ref, x_ref, o_ref = refs[0], refs[1], refs[-1]
    a = a_ref[...]
    hm = a.shape[0] // 2
    for cc in range(nchunk):
        sl = slice(cc * chunk, (cc + 1) * chunk)
        acc = _bdot(a, x_ref[0, :, sl])
        if mode == "kfmul":
            kf_ref = refs[2]
            xr, xi = acc[:hm], acc[hm:]
            kr, ki = kf_ref[0, :, sl], kf_ref[1, :, sl]
            o_ref[0, :hm, sl] = (xr * kr - xi * ki).astype(o_ref.dtype)
            o_ref[0, hm:, sl] = (xr * ki + xi * kr).astype(o_ref.dtype)
        else:
            o_ref[0, :, sl] = acc.astype(o_ref.dtype)


def _cmm(a, x, mode="plain", extra=(), out_dtype=BF16):
    mr, k = a.shape
    g, _, cols = x.shape
    tc = min(8192, cols)
    chunk = min(1024, tc)
    in_specs = [pl.BlockSpec((mr, k), lambda i, j: (0, 0)),
                pl.BlockSpec((1, k, tc), lambda i, j: (i, 0, j))]
    if mode == "kfmul":
        in_specs += [pl.BlockSpec((2, mr // 2, tc), lambda i, j: (0, 0, j))]
    return pl.pallas_call(
        functools.partial(_cmm_kernel, mode=mode, chunk=chunk, nchunk=tc // chunk),
        grid=(g, cols // tc),
        in_specs=in_specs,
        out_specs=pl.BlockSpec((1, mr, tc), lambda i, j: (i, 0, j)),
        out_shape=jax.ShapeDtypeStruct((g, mr, cols), out_dtype),
        compiler_params=_cparams(("arbitrary", "arbitrary")),
        name="dft_" + mode,
    )(a, x, *extra)


def _mid_filter_kernel(a_ref, t_ref, o_ref, *, kb, n2):
    for kk in range(kb):
        a = jnp.concatenate([a_ref[0, 0, kk], a_ref[0, 1, kk]], axis=0)
        x = _bdot(t_ref[kk], a)
        o_ref[0, kk] = x[:n2].astype(BF16)
        o_ref[1, kk] = x[n2:].astype(BF16)


def _fft_mid_filter(a5, consts):
    _, _, n1, n2, c = a5.shape
    kb, cs = min(8, n1), min(512, c)
    return pl.pallas_call(
        functools.partial(_mid_filter_kernel, kb=kb, n2=n2),
        grid=(n1 // kb, c // cs),
        in_specs=[pl.BlockSpec((1, 2, kb, n2, cs), lambda i, j: (0, 0, i, 0, j)),
                  pl.BlockSpec((kb, 2 * n2, 2 * n2), lambda i, j: (i, 0, 0))],
        out_specs=pl.BlockSpec((2, kb, n2, cs), lambda i, j: (0, i, 0, j)),
        out_shape=jax.ShapeDtypeStruct((2, n1, n2, c), BF16),
        compiler_params=_cparams(("arbitrary", "arbitrary")),
        name="dft_mid_filter",
    )(a5, consts["t_fwd"])


def _filter_spectrum(taps, consts):
    n, c = taps.shape
    n1, n2 = consts["n1"], consts["n2"]
    x = taps.reshape(1, n1, n2 * c)
    if n2 == 1:
        return _cmm(consts["f1_real"], x, out_dtype=F32).reshape(2, n1, c)
    a = _cmm(consts["f1_real"], x)
    return _fft_mid_filter(a.reshape(1, 2, n1, n2, c), consts)


def _conv_kernel(u_ref, kf_ref, f1_ref, t_ref, g1_ref, o_ref, s1, ab, *, n1, n2, nb):
    nin = n1 // 2
    npair = nb // 2
    lanes = lambda f: jnp.concatenate([f(s) for s in range(npair)], axis=1)

    for bi in range(nb):
        def fill(i, carry, bi=bi):
            src = pl.ds(pl.multiple_of(i * n2, n2), n2)
            dst = pl.ds(pl.multiple_of(i * FFT_PITCH, 8), n2)
            s1[bi % 2, bi // 2, dst, :] = u_ref[bi, src, :].astype(F32)
            return carry
        lax.fori_loop(0, nin, fill, 0, unroll=8)

    def stage1(j, carry):
        rows = pl.ds(j, nin, stride=FFT_PITCH)
        z = jnp.concatenate([lanes(lambda s: s1[0, s, rows, :]),
                             lanes(lambda s: s1[1, s, rows, :])], axis=0).astype(BF16)
        a = pltpu.bitcast(_bdot(f1_ref[...], z).astype(BF16), jnp.uint32)
        for s in range(npair):
            ab[s, pl.ds(j, n1, stride=FFT_PITCH), :] = a[:, s * LANES:(s + 1) * LANES]
        return carry

    lax.fori_loop(0, n2, stage1, 0, unroll=FFT_UNROLL)

    def mid(k, carry):
        rows = pl.ds(pl.multiple_of(k * FFT_PITCH, 8), n2)
        a = pltpu.bitcast(lanes(lambda s: ab[s, rows, :]), BF16)
        t = t_ref[k]
        x = _bdot(t, a)
        xr, xi = x[:n2], x[n2:]
        kr = lanes(lambda s: kf_ref[0, k].astype(F32))
        ki = lanes(lambda s: kf_ref[1, k].astype(F32))
        y = jnp.concatenate([xr * kr - xi * ki, xr * ki + xi * kr], axis=0).astype(BF16)
        bm = lax.dot_general(t, y, (((0,), (0,)), ((), ())), preferred_element_type=F32)
        w = pltpu.bitcast(bm.astype(BF16), jnp.uint32)
        for s in range(npair):
            ab[s, rows, :] = w[:, s * LANES:(s + 1) * LANES]
        return carry

    lax.fori_loop(0, n1, mid, 0, unroll=FFT_UNROLL)

    def stage4(j, carry):
        b = pltpu.bitcast(lanes(lambda s: ab[s, pl.ds(j, n1, stride=FFT_PITCH), :]), BF16)
        y = _bdot(g1_ref[...], b)
        rows = pl.ds(j, nin, stride=FFT_PITCH)
        for s in range(npair):
            s1[0, s, rows, :] = y[:nin, s * LANES:(s + 1) * LANES]
            s1[1, s, rows, :] = y[nin:, s * LANES:(s + 1) * LANES]
        return carry

    lax.fori_loop(0, n2, stage4, 0, unroll=FFT_UNROLL)

    for bi in range(nb):
        def emit(i, carry, bi=bi):
            dst = pl.ds(pl.multiple_of(i * n2, n2), n2)
            src = pl.ds(pl.multiple_of(i * FFT_PITCH, 8), n2)
            o_ref[bi, dst, :] = s1[bi % 2, bi // 2, src, :].astype(BF16)
            return carry
        lax.fori_loop(0, nin, emit, 0, unroll=8)


def _long_conv(u, kf, consts):
    b, l, c = u.shape
    n1, n2, nin = consts["n1"], consts["n2"], consts["nin"]
    p = b // 2
    if n2 == 1:
        x = u.reshape(p, 2 * nin, c)
        y = _cmm(consts["f1"], x, mode="kfmul", extra=(kf,))
        return _cmm(consts["g1"], y).reshape(b, l, c)
    const = lambda shape: pl.BlockSpec(shape, lambda j: (0,) * len(shape))
    return pl.pallas_call(
        functools.partial(_conv_kernel, n1=n1, n2=n2, nb=b),
        grid=(c // LANES,),
        in_specs=[pl.BlockSpec((b, l, LANES), lambda j: (0, 0, j)),
                  pl.BlockSpec((2, n1, n2, LANES), lambda j: (0, 0, 0, j)),
                  const((2 * n1, n1)), const((n1, 2 * n2, 2 * n2)), const((n1, 2 * n1))],
        out_specs=pl.BlockSpec((b, l, LANES), lambda j: (0, 0, j)),
        out_shape=jax.ShapeDtypeStruct((b, l, c), BF16),
        scratch_shapes=[pltpu.VMEM((2, p, nin * FFT_PITCH, LANES), F32),
                        pltpu.VMEM((p, n1 * FFT_PITCH, LANES), jnp.uint32)],
        compiler_params=_cparams(("arbitrary",)),
        name="long_conv",
    )(u, kf, consts["f1_il"], consts["t_il"], consts["g1_il"])


def _ret_kernel(*refs, nch, cl, states_only):
    if states_only:
        rd_ref, k_ref, v_ref, s0f_ref, s0b_ref, sf_ref, sb_ref, s_scr = refs
    else:
        (rd_ref, q_ref, k_ref, v_ref, g_ref, s0f_ref, s0b_ref,
         o_ref, sf_ref, sb_ref, s_scr, sball_scr, dmask_scr) = refs
    hd = pl.program_id(1)
    one = jnp.ones((1, 1), F32)
    lgf = -jnp.exp(one * rd_ref[0, hd])
    lgb = -jnp.exp(one * rd_ref[1, hd])
    ri = lax.broadcasted_iota(jnp.int32, (cl, 1), 0).astype(F32)
    ci = lax.broadcasted_iota(jnp.int32, (1, cl), 1).astype(F32)
    k_dec_f = jnp.exp(lgf * (cl - 1.0 - ci))
    k_dec_b = jnp.exp(lgb * ci)
    chunk_dec_f = jnp.exp(lgf * cl)
    chunk_dec_b = jnp.exp(lgb * cl)

    def k_transposed(k, dec):
        return (k.astype(F32).T * dec).astype(BF16)

    def load(ref, c):
        return ref[0, 0, pl.ds(pl.multiple_of(c * cl, cl), cl), :]

    s_scr[...] = s0b_ref[0, 0]

    def bstep(t, carry):
        c = nch - 1 - t
        if not states_only:
            sball_scr[c] = s_scr[...].astype(BF16)
        s_scr[...] = (s_scr[...] * chunk_dec_b
                      + _bdot(k_transposed(load(k_ref, c), k_dec_b), load(v_ref, c)))
        return carry

    lax.fori_loop(0, nch, bstep, 0)
    sb_ref[0, 0] = s_scr[...]

    s_scr[...] = s0f_ref[0, 0]
    if not states_only:
        q_dec_f = jnp.exp(lgf * (ri + 1.0))
        q_dec_b = jnp.exp(lgb * (cl - ri))
        diff = ri - ci
        dmask_scr[...] = (jnp.where(diff >= 0, jnp.exp(lgf * jnp.maximum(diff, 0.0)), 0.0)
                          + jnp.where(diff <= 0, jnp.exp(lgb * jnp.maximum(-diff, 0.0)), 0.0))

    def fstep(c, carry):
        k = load(k_ref, c)
        v = load(v_ref, c)
        if not states_only:
            q = load(q_ref, c)
            scores = lax.dot_general(q, k, (((1,), (1,)), ((), ())), preferred_element_type=F32)
            o = _bdot((scores * dmask_scr[...]).astype(BF16), v)
            qf = q.astype(F32)
            q2 = jnp.concatenate([(qf * q_dec_f).astype(BF16), (qf * q_dec_b).astype(BF16)], axis=1)
            s2 = jnp.concatenate([s_scr[...].astype(BF16), sball_scr[c]], axis=0)
            o = o + _bdot(q2, s2)
            o = o * lax.rsqrt(jnp.mean(o * o, axis=-1, keepdims=True) + NORM_EPS)
            r0 = pl.multiple_of(c * cl, cl)
            o_ref[0, pl.ds(r0, cl), :] = (o * load(g_ref, c).astype(F32)).astype(BF16)
        s_scr[...] = s_scr[...] * chunk_dec_f + _bdot(k_transposed(k, k_dec_f), v)
        return carry

    lax.fori_loop(0, nch, fstep, 0)
    sf_ref[0, 0] = s_scr[...]


def _retention(qk, k_off, v, gate, ret_decay, s0f, s0b, b, states_only=False):
    h = v.shape[0]
    l = v.shape[1] // b
    cl = min(RET_CHUNK, l)
    nch = l // cl
    sspec = pl.BlockSpec((1, 1, QK_DIM, V_DIM), lambda i, j: (i, j, 0, 0))
    state_shape = jax.ShapeDtypeStruct((b, h, QK_DIM, V_DIM), F32)
    hspec = lambda w, off: pl.BlockSpec((1, 1, l, w), lambda i, j: (off + j, i, 0, 0))
    qk4 = qk.reshape(k_off + h, b, l, QK_DIM)
    v4 = v.reshape(h, b, l, V_DIM)
    smem = pl.BlockSpec(memory_space=pltpu.SMEM)
    kern = functools.partial(_ret_kernel, nch=nch, cl=cl, states_only=states_only)
    if states_only:
        return pl.pallas_call(
            kern, grid=(b, h),
            in_specs=[smem, hspec(QK_DIM, k_off), hspec(V_DIM, 0), sspec, sspec],
            out_specs=[sspec, sspec],
            out_shape=[state_shape, state_shape],
            scratch_shapes=[pltpu.VMEM((QK_DIM, V_DIM), F32)],
            compiler_params=_cparams(("arbitrary", "arbitrary")),
            name="retention_states",
        )(ret_decay, qk4, v4, s0f, s0b)
    return pl.pallas_call(
        kern, grid=(b, h),
        in_specs=[smem, hspec(QK_DIM, 0), hspec(QK_DIM, k_off), hspec(V_DIM, 0), hspec(V_DIM, 0),
                  sspec, sspec],
        out_specs=[pl.BlockSpec((1, l, V_DIM), lambda i, j: (i, 0, j)), sspec, sspec],
        out_shape=[jax.ShapeDtypeStruct((b, l, h * V_DIM), BF16), state_shape, state_shape],
        scratch_shapes=[pltpu.VMEM((QK_DIM, V_DIM), F32),
                        pltpu.VMEM((nch, QK_DIM, V_DIM), BF16),
                        pltpu.VMEM((cl, cl), F32)],
        compiler_params=_cparams(("arbitrary", "arbitrary")),
        name="retention",
    )(ret_decay, qk4, qk4, v4, gate.reshape(h, b, l, V_DIM), s0f, s0b)


def _merge_kernel(y_ref, u_ref, m_ref, hb_ref, ret_ref, why_ref, wret_ref, ghy_ref, gret_ref,
                  o_ref, hy_scr):
    @pl.when(pl.program_id(1) == 0)
    def _():
        hy = (y_ref[...].astype(F32) + hb_ref[...] * u_ref[...].astype(F32)) * m_ref[...].astype(F32)
        hy_scr[...] = hy.astype(BF16)

    a = _bdot(hy_scr[...], why_ref[...])
    b = _bdot(ret_ref[...], wret_ref[...])
    o_ref[...] = (ghy_ref[...].astype(F32) * a + gret_ref[...].astype(F32) * b).astype(BF16)


def _merge(y, u, m, hy_bias, ret, pmg, w_hy, w_ret):
    mm, d = y.shape
    tm = min(512, mm)
    tn = min(512, d)
    nb = d // tn
    row = pl.BlockSpec((tm, d), lambda i, j: (i, 0))
    return pl.pallas_call(
        _merge_kernel,
        grid=(mm // tm, nb),
        in_specs=[
            row, row, row,
            pl.BlockSpec((1, d), lambda i, j: (0, 0)),
            pl.BlockSpec((tm, 2 * d), lambda i, j: (i, 0)),
            pl.BlockSpec((d, tn), lambda i, j: (0, j)),
            pl.BlockSpec((2 * d, tn), lambda i, j: (0, j)),
            pl.BlockSpec((tm, tn), lambda i, j: (i, j)),
            pl.BlockSpec((tm, tn), lambda i, j: (i, nb + j)),
        ],
        out_specs=pl.BlockSpec((tm, tn), lambda i, j: (i, j)),
        out_shape=jax.ShapeDtypeStruct((mm, d), BF16),
        scratch_shapes=[pltpu.VMEM((tm, d), BF16)],
        compiler_params=_cparams(("arbitrary", "arbitrary")),
        name="merge_proj",
    )(y, u, m, hy_bias.reshape(1, d), ret, w_hy, w_ret, pmg, pmg)


def _wo_kernel(*refs, final):
    if final:
        m_ref, w_ref, x_ref, g_ref, fg_ref, o_ref = refs
    else:
        m_ref, w_ref, x_ref, g_ref, o_ref = refs
    xn = x_ref[0] + g_ref[0] * _bdot(m_ref[0], w_ref[...])
    if final:
        xn = xn * lax.rsqrt(jnp.mean(xn * xn, axis=-1, keepdims=True) + NORM_EPS) * fg_ref[...]
    o_ref[0] = xn


def _wo_residual(mrg, w_o, x, gate, final_g=None):
    b, l, d = x.shape
    tl = min(512, l)
    final = final_g is not None
    in_specs = [
        pl.BlockSpec((1, tl, d), lambda i, j: (i, j, 0)),
        pl.BlockSpec((d, d), lambda i, j: (0, 0)),
        pl.BlockSpec((1, tl, d), lambda i, j: (i, j, 0)),
        pl.BlockSpec((1, 1, d), lambda i, j: (i, 0, 0)),
    ]
    args = [mrg.reshape(b, l, d), w_o, x, gate]
    if final:
        in_specs.append(pl.BlockSpec((1, d), lambda i, j: (0, 0)))
        args.append(final_g.reshape(1, d))
    return pl.pallas_call(
        functools.partial(_wo_kernel, final=final),
        grid=(b, l // tl),
        in_specs=in_specs,
        out_specs=pl.BlockSpec((1, tl, d), lambda i, j: (i, j, 0)),
        out_shape=jax.ShapeDtypeStruct((b, l, d), F32),
        compiler_params=_cparams(("arbitrary", "arbitrary")),
        name="wo_residual",
    )(*args)


def _mixer(h, use_rope, s0f, s0b, w_in, layer, conv_w, conv_b, kf, consts, hy_bias, ret_decay,
           w_hy_b, w_ret_b):
    b, l, d = h.shape
    h2 = h.reshape(b * l, d)
    proj = functools.partial(_proj, h2, w_in, layer)
    qk = proj(0, 2 * d, l, "rope" if use_rope else "qkscale", hw=QK_DIM)
    v = proj(2 * d, 2 * d, l, "plain", hw=V_DIM)
    rg = proj(4 * d, 2 * d, l, "silu", hw=V_DIM)
    phy = proj(6 * d, 3 * d, l, "plain")
    phg = proj(9 * d, d, l, "silu")
    pmg = proj(10 * d, 2 * d, l, "sigmoid")
    u, m = _hyena_front(phy.reshape(b, l, 3 * d), phg.reshape(b, l, d), conv_w, conv_b)
    y = _long_conv(u, kf, consts)
    ret, sf, sb = _retention(qk, d // QK_DIM, v, rg, ret_decay, s0f, s0b, b)
    flat = lambda a: a.reshape(b * l, -1)
    mrg = _merge(flat(y), flat(u), flat(m), hy_bias, flat(ret), pmg, w_hy_b, w_ret_b)
    return mrg, sf, sb


def kernel(x, c, ctx, c_ctx, ln_g, ada_w, ada_b, w_in, hy_conv_w, hy_conv_b, hy_filt_w1,
           hy_filt_b1, hy_filt_w2, hy_filt_b2, hy_filt_w3, hy_filt_b3, hy_filt_freq,
           hy_filt_wout, hy_bias, ret_decay, w_hy_out, w_ret_out, w_o, final_g):
    b, l, d = x.shape
    lc = ctx.shape[1]
    depth = ln_g.shape[0]
    h = d // QK_DIM
    assert b % 2 == 0 and d % QK_DIM == 0 and l % GRID_W == 0

    rows = -(-(b + 1) // 8) * 8
    cond = jnp.zeros((rows, d), F32).at[:b].set(c).at[b].set(c_ctx)
    mod = _ada_modulation(cond, ada_w, ada_b)

    consts_l = _dft_consts(l, FFT_N2 if (2 * l) % FFT_N2 == 0 and l >= 1024 else 1)
    consts_c = _dft_consts(lc, 1)
    zero_state = jnp.zeros((b, h, QK_DIM, V_DIM), F32)

    for i in range(depth):
        sh, sc, gt = mod[i, :, :d], mod[i, :, d:2 * d], mod[i, :, 2 * d:]
        lat = lambda a: a[:b, None, :]
        cx = lambda a: jnp.broadcast_to(a[b][None, None, :], (b, 1, d))
        w_hy_b = w_hy_out[i].astype(BF16)
        w_ret_b = w_ret_out[i].astype(BF16)
        w_o_b = w_o[i].astype(BF16)
        filt = (hy_filt_w1[i], hy_filt_b1[i], hy_filt_w2[i], hy_filt_b2[i], hy_filt_w3[i],
                hy_filt_b3[i], hy_filt_freq[i], hy_filt_wout[i])
        params = lambda kf, consts: (w_in, i, hy_conv_w[i], hy_conv_b[i], kf, consts, hy_bias[i],
                                     ret_decay[i], w_hy_b, w_ret_b)

        h_ctx = _prenorm(ctx, ln_g[i], cx(sc), cx(sh))
        if i < depth - 1:
            kf_c = _filter_spectrum(_hyena_filter_taps(lc, d, *filt), consts_c)
            mrg_c, s_ctx_f, s_ctx_b = _mixer(h_ctx, False, zero_state, zero_state,
                                             *params(kf_c, consts_c))
            ctx_next = _wo_residual(mrg_c, w_o_b, ctx, cx(gt))
        else:
            hc2 = h_ctx.reshape(b * lc, d)
            k_c = _proj(hc2, w_in, i, d, d, lc, "qkscale", hw=QK_DIM)
            v_c = _proj(hc2, w_in, i, 2 * d, 2 * d, lc, "plain", hw=V_DIM)
            s_ctx_f, s_ctx_b = _retention(k_c, 0, v_c, None, ret_decay[i], zero_state, zero_state,
                                          b, states_only=True)
            ctx_next = ctx

        h_lat = _prenorm(x, ln_g[i], lat(sc), lat(sh))
        kf_l = _filter_spectrum(_hyena_filter_taps(l, d, *filt), consts_l)
        mrg, _, _ = _mixer(h_lat, True, s_ctx_f, s_ctx_b, *params(kf_l, consts_l))
        x = _wo_residual(mrg, w_o_b, x, lat(gt), final_g if i == depth - 1 else None)
        ctx = ctx_next

    return x
```

```python
import functools
import math

import numpy as np
import jax
import jax.numpy as jnp
from jax import lax
from jax.experimental import pallas as pl
from jax.experimental.pallas import tpu as pltpu

F32 = jnp.float32
BF16 = jnp.bfloat16

NORM_EPS = 1e-6
GRID_W = 64
ROPE_BASE = 10000.0
QK_DIM = 256
V_DIM = 512
RET_CHUNK = 256
RET_UNROLL = 2
HY_EMB = 33
HY_BANDS = (HY_EMB - 1) // 2
HY_EMB_PAD = 64
HY_FAST_DECAY_PCT = 0.3
HY_SLOW_DECAY_PCT = 1.5
HY_DECAY_TARGET = 1e-2
FFT_N2 = 64
PROJ_NSUB = 4
LANES = 128
FFT_PITCH = 72
FFT_UNROLL = 16
VMEM_LIMIT = 56 * 1024 * 1024


def _cparams(sem):
    return pltpu.CompilerParams(dimension_semantics=sem, vmem_limit_bytes=VMEM_LIMIT)


def _silu(x):
    return x / (1.0 + jnp.exp(-x))


def _sigmoid(x):
    return 1.0 / (1.0 + jnp.exp(-x))


def _bdot(a, b):
    return jnp.dot(a, b, preferred_element_type=F32)


def _ada_kernel(c_ref, w_ref, b_ref, o_ref):
    s = _silu(c_ref[...]).astype(BF16)
    o_ref[0] = _bdot(s, w_ref[0].astype(BF16)) + b_ref[0]


def _ada_modulation(cond, ada_w, ada_b):
    depth, d, w3 = ada_w.shape
    r = cond.shape[0]
    tn = min(512, w3)
    return pl.pallas_call(
        _ada_kernel,
        grid=(depth, w3 // tn),
        in_specs=[
            pl.BlockSpec((r, d), lambda l, j: (0, 0)),
            pl.BlockSpec((1, d, tn), lambda l, j: (l, 0, j)),
            pl.BlockSpec((1, 1, tn), lambda l, j: (l, 0, j)),
        ],
        out_specs=pl.BlockSpec((1, r, tn), lambda l, j: (l, 0, j)),
        out_shape=jax.ShapeDtypeStruct((depth, r, w3), F32),
        compiler_params=_cparams(("arbitrary", "arbitrary")),
        name="ada_mod",
    )(cond, ada_w, ada_b.reshape(depth, 1, w3))


def _prenorm_kernel(x_ref, g_ref, sc_ref, sh_ref, o_ref):
    x = x_ref[0]
    y = x * lax.rsqrt(jnp.mean(x * x, axis=-1, keepdims=True) + NORM_EPS)
    o_ref[0] = ((y * g_ref[...]) * (1.0 + sc_ref[0]) + sh_ref[0]).astype(BF16)


def _prenorm(x, g, sc, sh):
    b, l, d = x.shape
    tl = min(512, l)
    return pl.pallas_call(
        _prenorm_kernel,
        grid=(b, l // tl),
        in_specs=[
            pl.BlockSpec((1, tl, d), lambda i, j: (i, j, 0)),
            pl.BlockSpec((1, d), lambda i, j: (0, 0)),
            pl.BlockSpec((1, 1, d), lambda i, j: (i, 0, 0)),
            pl.BlockSpec((1, 1, d), lambda i, j: (i, 0, 0)),
        ],
        out_specs=pl.BlockSpec((1, tl, d), lambda i, j: (i, j, 0)),
        out_shape=jax.ShapeDtypeStruct((b, l, d), BF16),
        compiler_params=_cparams(("arbitrary", "arbitrary")),
        name="prenorm",
    )(x, g.reshape(1, d), sc, sh)


def _proj_kernel(*refs, epi, nsub, tm, tn, r, hw, j0):
    if epi == "rope":
        h_ref, w_ref, cos_ref, sin_ref, o_ref, wb_scr = refs
    else:
        h_ref, w_ref, o_ref, wb_scr = refs

    @pl.when(pl.program_id(1) == 0)
    def _():
        wb_scr[...] = w_ref[...].astype(BF16)

    qk_scale = jnp.where(j0 + pl.program_id(0) < r, 1.0, QK_DIM ** -0.5).astype(F32)
    ts = tm // nsub
    sw = 128 if epi == "rope" else (hw or tn)
    for s in range(nsub):
        rows = slice(s * ts, (s + 1) * ts)
        acc = _bdot(h_ref[rows, :], wb_scr[...])
        for g in range(tn // sw):
            val = acc[:, g * sw:(g + 1) * sw]
            if epi == "rope":
                t = (g % 2) * 128
                val = (val * cos_ref[rows, t:t + 128]
                       + pltpu.roll(val, 64, 1) * sin_ref[rows, t:t + 128]) * qk_scale
            elif epi == "qkscale":
                val = val * qk_scale
            elif epi == "silu":
                val = _silu(val)
            elif epi == "sigmoid":
                val = _sigmoid(val)
            val = val.astype(BF16)
            if hw:
                c0 = g * sw
                o_ref[c0 // hw, rows, c0 % hw:c0 % hw + sw] = val
            else:
                o_ref[rows, g * sw:(g + 1) * sw] = val


def _rope_tables(l):
    quarter = QK_DIM // 4
    inv = 1.0 / (ROPE_BASE ** (jnp.arange(quarter, dtype=F32) / quarter))
    t = jnp.arange(l)
    row = (t // GRID_W).astype(F32)
    col = (t % GRID_W).astype(F32)
    ar = row[:, None] * inv[None, :]
    ac = col[:, None] * inv[None, :]
    cos_t = jnp.concatenate([jnp.cos(ar), jnp.cos(ar), jnp.cos(ac), jnp.cos(ac)], axis=-1)
    sin_t = jnp.concatenate([-jnp.sin(ar), jnp.sin(ar), -jnp.sin(ac), jnp.sin(ac)], axis=-1)
    return cos_t, sin_t


def _proj(h, w, layer, col0, ncols, l, epi, hw=None):
    m, d = h.shape
    tm = min(1024, m)
    tn = min(1024, d)
    j0 = col0 // tn
    assert epi != "rope" or l % tm == 0
    in_specs = [
        pl.BlockSpec((tm, d), lambda j, i: (i, 0)),
        pl.BlockSpec((None, d, tn), lambda j, i: (layer, 0, j0 + j)),
    ]
    args = [h, w]
    if epi == "rope":
        lb = l // tm
        in_specs += [pl.BlockSpec((tm, QK_DIM), lambda j, i: (i % lb, 0))] * 2
        args += list(_rope_tables(l))
    if hw:
        out_spec = pl.BlockSpec((tn // hw, tm, hw), lambda j, i: (j, i, 0))
        out_shape = jax.ShapeDtypeStruct((ncols // hw, m, hw), BF16)
    else:
        out_spec = pl.BlockSpec((tm, tn), lambda j, i: (i, j))
        out_shape = jax.ShapeDtypeStruct((m, ncols), BF16)
    return pl.pallas_call(
        functools.partial(_proj_kernel, epi=epi, nsub=PROJ_NSUB, tm=tm, tn=tn, r=d // tn, hw=hw,
                          j0=j0),
        grid=(ncols // tn, m // tm),
        in_specs=in_specs,
        out_specs=out_spec,
        out_shape=out_shape,
        scratch_shapes=[pltpu.VMEM((d, tn), BF16)],
        compiler_params=_cparams(("arbitrary", "arbitrary")),
        name="in_proj_" + epi,
    )(*args)


def _hyfront_kernel(v_ref, x0_ref, x1_ref, g_ref, wv_ref, w0_ref, w1_ref,
                    bv_ref, b0_ref, b1_ref, u_ref, m_ref, *, l):
    rows = lax.broadcasted_iota(jnp.int32, (l, 1), 0)

    def conv3(x_ref, w_ref, b_ref):
        x = x_ref[0].astype(F32)
        prev = jnp.where(rows == 0, 0.0, pltpu.roll(x, 1, 0))
        nxt = jnp.where(rows == l - 1, 0.0, pltpu.roll(x, l - 1, 0))
        return prev * w_ref[0:1, :] + x * w_ref[1:2, :] + nxt * w_ref[2:3, :] + b_ref[...]

    hv = conv3(v_ref, wv_ref, bv_ref)
    hx1 = conv3(x1_ref, w1_ref, b1_ref)
    u_ref[0] = (hv * hx1).astype(BF16)
    hx0 = conv3(x0_ref, w0_ref, b0_ref)
    m_ref[0] = (hx0 * g_ref[0].astype(F32)).astype(BF16)


def _hyena_front(phy, phg, conv_w, conv_b):
    b, l, d = phg.shape
    cs = 128
    nb = d // cs
    pspec = lambda off: pl.BlockSpec((1, l, cs), lambda i, j: (i, 0, off * nb + j))
    wspec = lambda off: pl.BlockSpec((3, cs), lambda i, j: (0, off * nb + j))
    bspec = lambda off: pl.BlockSpec((1, cs), lambda i, j: (0, off * nb + j))
    ospec = pl.BlockSpec((1, l, cs), lambda i, j: (i, 0, j))
    return pl.pallas_call(
        functools.partial(_hyfront_kernel, l=l),
        grid=(b, nb),
        in_specs=[pspec(0), pspec(1), pspec(2), pspec(0),
                  wspec(0), wspec(1), wspec(2), bspec(0), bspec(1), bspec(2)],
        out_specs=[ospec, ospec],
        out_shape=[jax.ShapeDtypeStruct((b, l, d), BF16)] * 2,
        compiler_params=_cparams(("arbitrary", "arbitrary")),
        name="hyena_front",
    )(phy, phy, phy, phg, conv_w, conv_w, conv_w,
      conv_b.reshape(1, -1), conv_b.reshape(1, -1), conv_b.reshape(1, -1))


def _filter_kernel(z_ref, t_ref, w1_ref, b1_ref, w2_ref, b2_ref, w3_ref, b3_ref, fr_ref,
                   wo_ref, dl_ref, o_ref, *, tr, l):
    dot = functools.partial(jnp.dot, precision=lax.Precision.HIGHEST, preferred_element_type=F32)
    f = fr_ref[...]
    h = jnp.sin(f * (dot(z_ref[...], w1_ref[...]) + b1_ref[...]))
    h = jnp.sin(f * (dot(h, w2_ref[...]) + b2_ref[...]))
    h = jnp.sin(f * (dot(h, w3_ref[...]) + b3_ref[...]))
    y = dot(h, wo_ref[...])
    win = jnp.exp(-t_ref[...] * dl_ref[...])
    rows = pl.program_id(0) * tr + lax.broadcasted_iota(jnp.int32, (tr, 1), 0)
    o_ref[...] = jnp.where(rows == l, 0.0, y * win).astype(BF16)


def _hyena_filter_taps(l, c, w1, b1, w2, b2, w3, b3, freq, w_out):
    t = jnp.linspace(0.0, 1.0, l, dtype=F32)[:, None]
    ang = 2.0 * math.pi * jnp.arange(l, dtype=F32)[:, None] / l
    f = jnp.linspace(1e-4, HY_BANDS - 1, HY_BANDS, dtype=F32)[None, :]
    z = jnp.concatenate([t, jnp.cos(f * ang), -jnp.sin(f * ang)], axis=-1)
    z = jnp.pad(z, ((0, 0), (0, HY_EMB_PAD - HY_EMB)))
    back = lambda a: jnp.concatenate([a[l - 1:l], a[:0:-1]], axis=0)
    z2 = jnp.concatenate([z, back(z)], axis=0)
    t2 = jnp.concatenate([t, back(t)], axis=0)
    max_decay = math.log(HY_DECAY_TARGET) / HY_FAST_DECAY_PCT
    min_decay = math.log(HY_DECAY_TARGET) / HY_SLOW_DECAY_PCT
    deltas = jnp.abs(jnp.linspace(min_decay, max_decay, c, dtype=F32))[None, :]
    w1p = jnp.pad(w1, ((0, HY_EMB_PAD - HY_EMB), (0, 0)))
    hid = w2.shape[0]
    tr = min(512, l)
    lb = l // tr
    full = lambda shape: pl.BlockSpec(shape, lambda i: (0, 0))
    return pl.pallas_call(
        functools.partial(_filter_kernel, tr=tr, l=l),
        grid=(2 * lb,),
        in_specs=[
            pl.BlockSpec((tr, HY_EMB_PAD), lambda i: (i, 0)),
            pl.BlockSpec((tr, 1), lambda i: (i, 0)),
            full((HY_EMB_PAD, hid)), full((1, hid)),
            full((hid, hid)), full((1, hid)),
            full((hid, hid)), full((1, hid)),
            full((1, hid)),
            pl.BlockSpec((hid, c), lambda i: (0, i // lb)),
            full((1, c)),
        ],
        out_specs=pl.BlockSpec((tr, c), lambda i: (i, 0)),
        out_shape=jax.ShapeDtypeStruct((2 * l, c), BF16),
        compiler_params=_cparams(("arbitrary",)),
        name="hyena_filter",
    )(z2, t2, w1p, b1.reshape(1, -1), w2, b2.reshape(1, -1), w3, b3.reshape(1, -1),
      freq.reshape(1, -1), w_out, deltas)


def _dft_consts(l, n2):
    n = 2 * l
    n1 = n // n2
    nin = n1 // 2

    def cs(k, m, period):
        ph = 2.0 * np.pi * ((np.outer(k, m)) % period) / period
        return np.cos(ph), np.sin(ph)

    k1 = np.arange(n1)
    c, s = cs(k1, np.arange(nin), n1)
    f1 = np.block([[c, s], [-s, c]])
    cf, sf = cs(k1, np.arange(n1), n1)
    f1_real = np.concatenate([cf, -sf], axis=0)
    ci, si = cs(np.arange(nin), k1, n1)
    g1 = np.block([[ci, -si], [si, ci]]) / n
    out = dict(n1=n1, n2=n2, nin=nin,
               f1=jnp.asarray(f1, F32).astype(BF16),
               f1_real=jnp.asarray(f1_real, F32).astype(BF16),
               g1=jnp.asarray(g1, F32).astype(BF16))
    if n2 > 1:
        f1_il = np.empty_like(f1)
        f1_il[0::2], f1_il[1::2] = f1[:n1], f1[n1:]
        out["f1_il"] = jnp.asarray(f1_il, F32).astype(BF16)
        out["g1_il"] = jnp.asarray(f1_il.T / n, F32).astype(BF16)
        f1r_il = np.empty_like(f1_real)
        f1r_il[0::2], f1r_il[1::2] = f1_real[:n1], f1_real[n1:]
        out["f1_real_il"] = jnp.asarray(f1r_il, F32).astype(BF16)
        m2 = np.arange(n2)
        ph = (m2[None, None, :] * k1[:, None, None] + n1 * m2[None, None, :] * m2[None, :, None]) % n
        ph = 2.0 * np.pi * ph / n
        tr, ti = np.cos(ph), -np.sin(ph)
        t_fwd = np.concatenate([np.concatenate([tr, -ti], axis=2),
                                np.concatenate([ti, tr], axis=2)], axis=1)
        t_il = np.empty_like(t_fwd)
        t_il[:, :, 0::2], t_il[:, :, 1::2] = t_fwd[:, :, :n2], t_fwd[:, :, n2:]
        out["t_il"] = jnp.asarray(t_il, F32).astype(BF16)
    return out


def _cmm_kernel(*refs, mode, chunk, nchunk):
    a_ref, x_ref, o_ref = refs[0], refs[1], refs[-1]
    a = a_ref[...]
    hm = a.shape[0] // 2
    for cc in range(nchunk):
        sl = slice(cc * chunk, (cc + 1) * chunk)
        acc = _bdot(a, x_ref[0, :, sl])
        if mode == "kfmul":
            kf_ref = refs[2]
            xr, xi = acc[:hm], acc[hm:]
            kr, ki = kf_ref[0, :, sl], kf_ref[1, :, sl]
            o_ref[0, :hm, sl] = (xr * kr - xi * ki).astype(o_ref.dtype)
            o_ref[0, hm:, sl] = (xr * ki + xi * kr).astype(o_ref.dtype)
        elif mode == "epi":
            u_ref, m_ref, b_ref = refs[2], refs[3], refs[4]
            u = u_ref[0, :, sl].astype(F32)
            o = (acc + b_ref[:, sl] * u) * m_ref[0, :, sl].astype(F32)
            o_ref[0, :, sl] = o.astype(o_ref.dtype)
        else:
            o_ref[0, :, sl] = acc.astype(o_ref.dtype)


def _cmm(a, x, mode="plain", extra=(), out_dtype=BF16):
    mr, k = a.shape
    g, _, cols = x.shape
    tc = min(8192, cols)
    chunk = min(1024, tc)
    in_specs = [pl.BlockSpec((mr, k), lambda i, j: (0, 0)),
                pl.BlockSpec((1, k, tc), lambda i, j: (i, 0, j))]
    if mode == "kfmul":
        in_specs += [pl.BlockSpec((2, mr // 2, tc), lambda i, j: (0, 0, j))]
    elif mode == "epi":
        in_specs += [pl.BlockSpec((1, mr, tc), lambda i, j: (i, 0, j)),
                     pl.BlockSpec((1, mr, tc), lambda i, j: (i, 0, j)),
                     pl.BlockSpec((1, tc), lambda i, j: (0, j))]
    return pl.pallas_call(
        functools.partial(_cmm_kernel, mode=mode, chunk=chunk, nchunk=tc // chunk),
        grid=(g, cols // tc),
        in_specs=in_specs,
        out_specs=pl.BlockSpec((1, mr, tc), lambda i, j: (i, 0, j)),
        out_shape=jax.ShapeDtypeStruct((g, mr, cols), out_dtype),
        compiler_params=_cparams(("arbitrary", "arbitrary")),
        name="dft_" + mode,
    )(a, x, *extra)


def _filter_fft_kernel(x_ref, f1_ref, t_ref, o_ref, s1, ab, *, n1, n2):
    def fill(i, carry):
        src = pl.ds(pl.multiple_of(i * n2, n2), n2)
        s1[pl.ds(pl.multiple_of(i * FFT_PITCH, 8), n2), :] = x_ref[src, :].astype(F32)
        return carry

    lax.fori_loop(0, n1, fill, 0, unroll=8)

    def stage1(j, carry):
        rows = pl.ds(j, n1, stride=FFT_PITCH)
        z = s1[rows, :].astype(BF16)
        ab[rows, :] = pltpu.bitcast(_bdot(f1_ref[...], z).astype(BF16), jnp.uint32)
        return carry

    lax.fori_loop(0, n2, stage1, 0, unroll=FFT_UNROLL)

    def mid(k, carry):
        a = pltpu.bitcast(ab[pl.ds(pl.multiple_of(k * FFT_PITCH, 8), n2), :], BF16)
        x = _bdot(t_ref[k], a)
        o_ref[0, k] = x[:n2].astype(BF16)
        o_ref[1, k] = x[n2:].astype(BF16)
        return carry

    lax.fori_loop(0, n1, mid, 0, unroll=FFT_UNROLL)


def _filter_spectrum(taps, consts):
    n, c = taps.shape
    n1, n2 = consts["n1"], consts["n2"]
    if n2 == 1:
        x = taps.reshape(1, n1, c)
        return _cmm(consts["f1_real"], x, out_dtype=F32).reshape(2, n1, c)
    const = lambda shape: pl.BlockSpec(shape, lambda j: (0,) * len(shape))
    return pl.pallas_call(
        functools.partial(_filter_fft_kernel, n1=n1, n2=n2),
        grid=(c // LANES,),
        in_specs=[pl.BlockSpec((n, LANES), lambda j: (0, j)),
                  const((2 * n1, n1)), const((n1, 2 * n2, 2 * n2))],
        out_specs=pl.BlockSpec((2, n1, n2, LANES), lambda j: (0, 0, 0, j)),
        out_shape=jax.ShapeDtypeStruct((2, n1, n2, c), BF16),
        scratch_shapes=[pltpu.VMEM((n1 * FFT_PITCH, LANES), F32),
                        pltpu.VMEM((n1 * FFT_PITCH, LANES), jnp.uint32)],
        compiler_params=_cparams(("arbitrary",)),
        name="filter_fft",
    )(taps, consts["f1_real_il"], consts["t_il"])


def _conv_kernel(u_ref, m_ref, hb_ref, kf_ref, f1_ref, t_ref, g1_ref, o_ref, s1, ab, *, n1, n2, nb):
    nin = n1 // 2
    npair = nb // 2
    lanes = lambda f: jnp.concatenate([f(s) for s in range(npair)], axis=1)

    for bi in range(nb):
        def fill(i, carry, bi=bi):
            src = pl.ds(pl.multiple_of(i * n2, n2), n2)
            dst = pl.ds(pl.multiple_of(i * FFT_PITCH, 8), n2)
            s1[bi % 2, bi // 2, dst, :] = u_ref[bi, src, :].astype(F32)
            return carry
        lax.fori_loop(0, nin, fill, 0, unroll=8)

    def stage1(j, carry):
        rows = pl.ds(j, nin, stride=FFT_PITCH)
        z = jnp.concatenate([lanes(lambda s: s1[0, s, rows, :]),
                             lanes(lambda s: s1[1, s, rows, :])], axis=0).astype(BF16)
        a = pltpu.bitcast(_bdot(f1_ref[...], z).astype(BF16), jnp.uint32)
        for s in range(npair):
            ab[s, pl.ds(j, n1, stride=FFT_PITCH), :] = a[:, s * LANES:(s + 1) * LANES]
        return carry

    lax.fori_loop(0, n2, stage1, 0, unroll=FFT_UNROLL)

    def mid(k, carry):
        rows = pl.ds(pl.multiple_of(k * FFT_PITCH, 8), n2)
        a = pltpu.bitcast(lanes(lambda s: ab[s, rows, :]), BF16)
        t = t_ref[k]
        x = _bdot(t, a)
        xr, xi = x[:n2], x[n2:]
        kr = lanes(lambda s: kf_ref[0, k].astype(F32))
        ki = lanes(lambda s: kf_ref[1, k].astype(F32))
        y = jnp.concatenate([xr * kr - xi * ki, xr * ki + xi * kr], axis=0).astype(BF16)
        bm = lax.dot_general(t, y, (((0,), (0,)), ((), ())), preferred_element_type=F32)
        w = pltpu.bitcast(bm.astype(BF16), jnp.uint32)
        for s in range(npair):
            ab[s, rows, :] = w[:, s * LANES:(s + 1) * LANES]
        return carry

    lax.fori_loop(0, n1, mid, 0, unroll=FFT_UNROLL)

    def stage4(j, carry):
        b = pltpu.bitcast(lanes(lambda s: ab[s, pl.ds(j, n1, stride=FFT_PITCH), :]), BF16)
        y = _bdot(g1_ref[...], b)
        rows = pl.ds(j, nin, stride=FFT_PITCH)
        for s in range(npair):
            s1[0, s, rows, :] = y[:nin, s * LANES:(s + 1) * LANES]
            s1[1, s, rows, :] = y[nin:, s * LANES:(s + 1) * LANES]
        return carry

    lax.fori_loop(0, n2, stage4, 0, unroll=FFT_UNROLL)

    for bi in range(nb):
        def emit(i, carry, bi=bi):
            dst = pl.ds(pl.multiple_of(i * n2, n2), n2)
            src = pl.ds(pl.multiple_of(i * FFT_PITCH, 8), n2)
            y = s1[bi % 2, bi // 2, src, :] + hb_ref[...] * u_ref[bi, dst, :].astype(F32)
            o_ref[bi, dst, :] = (y * m_ref[bi, dst, :].astype(F32)).astype(BF16)
            return carry
        lax.fori_loop(0, nin, emit, 0, unroll=8)


def _long_conv(u, m, bias, kf, consts):
    b, l, c = u.shape
    n1, n2, nin = consts["n1"], consts["n2"], consts["nin"]
    p = b // 2
    if n2 == 1:
        x = u.reshape(p, 2 * nin, c)
        y = _cmm(consts["f1"], x, mode="kfmul", extra=(kf,))
        extra = (x, m.reshape(p, 2 * nin, c), bias.reshape(1, c))
        return _cmm(consts["g1"], y, mode="epi", extra=extra).reshape(b, l, c)
    const = lambda shape: pl.BlockSpec(shape, lambda j: (0,) * len(shape))
    return pl.pallas_call(
        functools.partial(_conv_kernel, n1=n1, n2=n2, nb=b),
        grid=(c // LANES,),
        in_specs=[pl.BlockSpec((b, l, LANES), lambda j: (0, 0, j)),
                  pl.BlockSpec((b, l, LANES), lambda j: (0, 0, j)),
                  pl.BlockSpec((1, LANES), lambda j: (0, j)),
                  pl.BlockSpec((2, n1, n2, LANES), lambda j: (0, 0, 0, j)),
                  const((2 * n1, n1)), const((n1, 2 * n2, 2 * n2)), const((n1, 2 * n1))],
        out_specs=pl.BlockSpec((b, l, LANES), lambda j: (0, 0, j)),
        out_shape=jax.ShapeDtypeStruct((b, l, c), BF16),
        scratch_shapes=[pltpu.VMEM((2, p, nin * FFT_PITCH, LANES), F32),
                        pltpu.VMEM((p, n1 * FFT_PITCH, LANES), jnp.uint32)],
        compiler_params=_cparams(("arbitrary",)),
        name="long_conv",
    )(u, m, bias.reshape(1, c), kf, consts["f1_il"], consts["t_il"], consts["g1_il"])


def _ret_kernel(*refs, nch, cl, states_only):
    if states_only:
        rd_ref, k_ref, v_ref, s0f_ref, s0b_ref, sf_ref, sb_ref, s_scr = refs
    else:
        (rd_ref, q_ref, k_ref, v_ref, g_ref, s0f_ref, s0b_ref,
         o_ref, sf_ref, sb_ref, s_scr, sball_scr, dmask_scr) = refs
    hd = pl.program_id(1)
    one = jnp.ones((1, 1), F32)
    lgf = -jnp.exp(one * rd_ref[0, hd])
    lgb = -jnp.exp(one * rd_ref[1, hd])
    ri = lax.broadcasted_iota(jnp.int32, (cl, 1), 0).astype(F32)
    ci = lax.broadcasted_iota(jnp.int32, (1, cl), 1).astype(F32)
    k_dec_f = jnp.exp(lgf * (cl - 1.0 - ri))
    k_dec_b = jnp.exp(lgb * ri)
    chunk_dec_f = jnp.exp(lgf * cl)
    chunk_dec_b = jnp.exp(lgb * cl)

    def kv_outer(k, dec, v):
        kd = (k.astype(F32) * dec).astype(BF16)
        return lax.dot_general(kd, v, (((0,), (0,)), ((), ())), preferred_element_type=F32)

    def load(ref, c):
        return ref[0, 0, pl.ds(pl.multiple_of(c * cl, cl), cl), :]

    s_scr[...] = s0b_ref[0, 0]

    def bstep(t, carry):
        c = nch - 1 - t
        if not states_only:
            sball_scr[c] = s_scr[...].astype(BF16)
        s_scr[...] = (s_scr[...] * chunk_dec_b
                      + kv_outer(load(k_ref, c), k_dec_b, load(v_ref, c)))
        return carry

    lax.fori_loop(0, nch, bstep, 0, unroll=RET_UNROLL)
    sb_ref[0, 0] = s_scr[...]

    s_scr[...] = s0f_ref[0, 0]
    if not states_only:
        q_dec_f = jnp.exp(lgf * (ri + 1.0))
        q_dec_b = jnp.exp(lgb * (cl - ri))
        diff = ri - ci
        dmask_scr[...] = (jnp.where(diff >= 0, jnp.exp(lgf * jnp.maximum(diff, 0.0)), 0.0)
                          + jnp.where(diff <= 0, jnp.exp(lgb * jnp.maximum(-diff, 0.0)), 0.0))

    def fstep(c, carry):
        k = load(k_ref, c)
        v = load(v_ref, c)
        if not states_only:
            q = load(q_ref, c)
            scores = lax.dot_general(q, k, (((1,), (1,)), ((), ())), preferred_element_type=F32)
            o = _bdot((scores * dmask_scr[...]).astype(BF16), v)
            qf = q.astype(F32)
            q2 = jnp.concatenate([(qf * q_dec_f).astype(BF16), (qf * q_dec_b).astype(BF16)], axis=1)
            s2 = jnp.concatenate([s_scr[...].astype(BF16), sball_scr[c]], axis=0)
            o = o + _bdot(q2, s2)
            o = o * lax.rsqrt(jnp.mean(o * o, axis=-1, keepdims=True) + NORM_EPS)
            r0 = pl.multiple_of(c * cl, cl)
            o_ref[0, pl.ds(r0, cl), :] = (o * load(g_ref, c).astype(F32)).astype(BF16)
        s_scr[...] = s_scr[...] * chunk_dec_f + kv_outer(k, k_dec_f, v)
        return carry

    lax.fori_loop(0, nch, fstep, 0, unroll=RET_UNROLL)
    sf_ref[0, 0] = s_scr[...]


def _retention(qk, k_off, v, gate, ret_decay, s0f, s0b, b, states_only=False):
    h = v.shape[0]
    l = v.shape[1] // b
    cl = min(RET_CHUNK, l)
    nch = l // cl
    sspec = pl.BlockSpec((1, 1, QK_DIM, V_DIM), lambda i, j: (i, j, 0, 0))
    state_shape = jax.ShapeDtypeStruct((b, h, QK_DIM, V_DIM), F32)
    hspec = lambda w, off: pl.BlockSpec((1, 1, l, w), lambda i, j: (off + j, i, 0, 0))
    qk4 = qk.reshape(k_off + h, b, l, QK_DIM)
    v4 = v.reshape(h, b, l, V_DIM)
    smem = pl.BlockSpec(memory_space=pltpu.SMEM)
    kern = functools.partial(_ret_kernel, nch=nch, cl=cl, states_only=states_only)
    if states_only:
        return pl.pallas_call(
            kern, grid=(b, h),
            in_specs=[smem, hspec(QK_DIM, k_off), hspec(V_DIM, 0), sspec, sspec],
            out_specs=[sspec, sspec],
            out_shape=[state_shape, state_shape],
            scratch_shapes=[pltpu.VMEM((QK_DIM, V_DIM), F32)],
            compiler_params=_cparams(("arbitrary", "arbitrary")),
            name="retention_states",
        )(ret_decay, qk4, v4, s0f, s0b)
    return pl.pallas_call(
        kern, grid=(b, h),
        in_specs=[smem, hspec(QK_DIM, 0), hspec(QK_DIM, k_off), hspec(V_DIM, 0), hspec(V_DIM, 0),
                  sspec, sspec],
        out_specs=[pl.BlockSpec((1, l, V_DIM), lambda i, j: (i, 0, j)), sspec, sspec],
        out_shape=[jax.ShapeDtypeStruct((b, l, h * V_DIM), BF16), state_shape, state_shape],
        scratch_shapes=[pltpu.VMEM((QK_DIM, V_DIM), F32),
                        pltpu.VMEM((nch, QK_DIM, V_DIM), BF16),
                        pltpu.VMEM((cl, cl), F32)],
        compiler_params=_cparams(("arbitrary", "arbitrary")),
        name="retention",
    )(ret_decay, qk4, qk4, v4, gate.reshape(h, b, l, V_DIM), s0f, s0b)


def _merge_kernel(hy_ref, ret_ref, why_ref, wret_ref, ghy_ref, gret_ref, o_ref):
    a = _bdot(hy_ref[...], why_ref[...])
    b = _bdot(ret_ref[...], wret_ref[...])
    o_ref[...] = (ghy_ref[...].astype(F32) * a + gret_ref[...].astype(F32) * b).astype(BF16)


def _merge(hy, ret, pmg, w_hy, w_ret):
    mm, d = hy.shape
    tm = min(512, mm)
    tn = min(1024, d)
    nb = d // tn
    return pl.pallas_call(
        _merge_kernel,
        grid=(mm // tm, nb),
        in_specs=[
            pl.BlockSpec((tm, d), lambda i, j: (i, 0)),
            pl.BlockSpec((tm, 2 * d), lambda i, j: (i, 0)),
            pl.BlockSpec((d, tn), lambda i, j: (0, j)),
            pl.BlockSpec((2 * d, tn), lambda i, j: (0, j)),
            pl.BlockSpec((tm, tn), lambda i, j: (i, j)),
            pl.BlockSpec((tm, tn), lambda i, j: (i, nb + j)),
        ],
        out_specs=pl.BlockSpec((tm, tn), lambda i, j: (i, j)),
        out_shape=jax.ShapeDtypeStruct((mm, d), BF16),
        compiler_params=_cparams(("arbitrary", "arbitrary")),
        name="merge_proj",
    )(hy, ret, w_hy, w_ret, pmg, pmg)


def _wo_kernel(*refs, final):
    if final:
        m_ref, w_ref, x_ref, g_ref, fg_ref, o_ref = refs
    else:
        m_ref, w_ref, x_ref, g_ref, o_ref = refs
    xn = x_ref[0] + g_ref[0] * _bdot(m_ref[0], w_ref[...])
    if final:
        xn = xn * lax.rsqrt(jnp.mean(xn * xn, axis=-1, keepdims=True) + NORM_EPS) * fg_ref[...]
    o_ref[0] = xn


def _wo_residual(mrg, w_o, x, gate, final_g=None):
    b, l, d = x.shape
    tl = min(512, l)
    final = final_g is not None
    in_specs = [
        pl.BlockSpec((1, tl, d), lambda i, j: (i, j, 0)),
        pl.BlockSpec((d, d), lambda i, j: (0, 0)),
        pl.BlockSpec((1, tl, d), lambda i, j: (i, j, 0)),
        pl.BlockSpec((1, 1, d), lambda i, j: (i, 0, 0)),
    ]
    args = [mrg.reshape(b, l, d), w_o, x, gate]
    if final:
        in_specs.append(pl.BlockSpec((1, d), lambda i, j: (0, 0)))
        args.append(final_g.reshape(1, d))
    return pl.pallas_call(
        functools.partial(_wo_kernel, final=final),
        grid=(b, l // tl),
        in_specs=in_specs,
        out_specs=pl.BlockSpec((1, tl, d), lambda i, j: (i, j, 0)),
        out_shape=jax.ShapeDtypeStruct((b, l, d), F32),
        compiler_params=_cparams(("arbitrary", "arbitrary")),
        name="wo_residual",
    )(*args)


def _mixer(h, use_rope, s0f, s0b, w_in, layer, conv_w, conv_b, kf, consts, hy_bias, ret_decay,
           w_hy_b, w_ret_b):
    b, l, d = h.shape
    h2 = h.reshape(b * l, d)
    proj = functools.partial(_proj, h2, w_in, layer)
    qk = proj(0, 2 * d, l, "rope" if use_rope else "qkscale", hw=QK_DIM)
    v = proj(2 * d, 2 * d, l, "plain", hw=V_DIM)
    rg = proj(4 * d, 2 * d, l, "silu", hw=V_DIM)
    phy = proj(6 * d, 3 * d, l, "plain")
    phg = proj(9 * d, d, l, "silu")
    pmg = proj(10 * d, 2 * d, l, "sigmoid")
    u, m = _hyena_front(phy.reshape(b, l, 3 * d), phg.reshape(b, l, d), conv_w, conv_b)
    hy = _long_conv(u, m, hy_bias, kf, consts)
    ret, sf, sb = _retention(qk, d // QK_DIM, v, rg, ret_decay, s0f, s0b, b)
    mrg = _merge(hy.reshape(b * l, d), ret.reshape(b * l, 2 * d), pmg, w_hy_b, w_ret_b)
    return mrg, sf, sb


def kernel(x, c, ctx, c_ctx, ln_g, ada_w, ada_b, w_in, hy_conv_w, hy_conv_b, hy_filt_w1,
           hy_filt_b1, hy_filt_w2, hy_filt_b2, hy_filt_w3, hy_filt_b3, hy_filt_freq,
           hy_filt_wout, hy_bias, ret_decay, w_hy_out, w_ret_out, w_o, final_g):
    b, l, d = x.shape
    lc = ctx.shape[1]
    depth = ln_g.shape[0]
    h = d // QK_DIM
    assert b % 2 == 0 and d % QK_DIM == 0 and l % GRID_W == 0

    rows = -(-(b + 1) // 8) * 8
    cond = jnp.zeros((rows, d), F32).at[:b].set(c).at[b].set(c_ctx)
    mod = _ada_modulation(cond, ada_w, ada_b)

    consts_l = _dft_consts(l, FFT_N2 if (2 * l) % FFT_N2 == 0 and l >= 1024 else 1)
    consts_c = _dft_consts(lc, 1)
    zero_state = jnp.zeros((b, h, QK_DIM, V_DIM), F32)

    for i in range(depth):
        sh, sc, gt = mod[i, :, :d], mod[i, :, d:2 * d], mod[i, :, 2 * d:]
        lat = lambda a: a[:b, None, :]
        cx = lambda a: jnp.broadcast_to(a[b][None, None, :], (b, 1, d))
        w_hy_b = w_hy_out[i].astype(BF16)
        w_ret_b = w_ret_out[i].astype(BF16)
        w_o_b = w_o[i].astype(BF16)
        filt = (hy_filt_w1[i], hy_filt_b1[i], hy_filt_w2[i], hy_filt_b2[i], hy_filt_w3[i],
                hy_filt_b3[i], hy_filt_freq[i], hy_filt_wout[i])
        params = lambda kf, consts: (w_in, i, hy_conv_w[i], hy_conv_b[i], kf, consts, hy_bias[i],
                                     ret_decay[i], w_hy_b, w_ret_b)

        h_ctx = _prenorm(ctx, ln_g[i], cx(sc), cx(sh))
        if i < depth - 1:
            kf_c = _filter_spectrum(_hyena_filter_taps(lc, d, *filt), consts_c)
            mrg_c, s_ctx_f, s_ctx_b = _mixer(h_ctx, False, zero_state, zero_state,
                                             *params(kf_c, consts_c))
            ctx_next = _wo_residual(mrg_c, w_o_b, ctx, cx(gt))
        else:
            hc2 = h_ctx.reshape(b * lc, d)
            k_c = _proj(hc2, w_in, i, d, d, lc, "qkscale", hw=QK_DIM)
            v_c = _proj(hc2, w_in, i, 2 * d, 2 * d, lc, "plain", hw=V_DIM)
            s_ctx_f, s_ctx_b = _retention(k_c, 0, v_c, None, ret_decay[i], zero_state, zero_state,
                                          b, states_only=True)
            ctx_next = ctx

        h_lat = _prenorm(x, ln_g[i], lat(sc), lat(sh))
        kf_l = _filter_spectrum(_hyena_filter_taps(l, d, *filt), consts_l)
        mrg, _, _ = _mixer(h_lat, True, s_ctx_f, s_ctx_b, *params(kf_l, consts_l))
        x = _wo_residual(mrg, w_o_b, x, lat(gt), final_g if i == depth - 1 else None)
        ctx = ctx_next

    return x
```

```python
import functools
import math

import numpy as np
import jax
import jax.numpy as jnp
from jax import lax
from jax.experimental import pallas as pl
from jax.experimental.pallas import tpu as pltpu

F32 = jnp.float32
BF16 = jnp.bfloat16

NORM_EPS = 1e-6
GRID_W = 64
ROPE_BASE = 10000.0
QK_DIM = 256
V_DIM = 512
RET_CHUNK = 256
RET_UNROLL = 2
HY_EMB = 33
HY_BANDS = (HY_EMB - 1) // 2
HY_EMB_PAD = 64
HY_FAST_DECAY_PCT = 0.3
HY_SLOW_DECAY_PCT = 1.5
HY_DECAY_TARGET = 1e-2
FFT_N2 = 64
PROJ_NSUB = 4
LANES = 128
FFT_PITCH = 72
FFT_UNROLL = 16
VMEM_LIMIT = 56 * 1024 * 1024


def _cparams(sem):
    return pltpu.CompilerParams(dimension_semantics=sem, vmem_limit_bytes=VMEM_LIMIT)


def _silu(x):
    return x / (1.0 + jnp.exp(-x))


def _sigmoid(x):
    return 1.0 / (1.0 + jnp.exp(-x))


def _bdot(a, b):
    return jnp.dot(a, b, preferred_element_type=F32)


def _ada_kernel(c_ref, w_ref, b_ref, o_ref):
    s = _silu(c_ref[...]).astype(BF16)
    o_ref[0] = _bdot(s, w_ref[0].astype(BF16)) + b_ref[0]


def _ada_modulation(cond, ada_w, ada_b):
    depth, d, w3 = ada_w.shape
    r = cond.shape[0]
    tn = min(512, w3)
    return pl.pallas_call(
        _ada_kernel,
        grid=(depth, w3 // tn),
        in_specs=[
            pl.BlockSpec((r, d), lambda l, j: (0, 0)),
            pl.BlockSpec((1, d, tn), lambda l, j: (l, 0, j)),
            pl.BlockSpec((1, 1, tn), lambda l, j: (l, 0, j)),
        ],
        out_specs=pl.BlockSpec((1, r, tn), lambda l, j: (l, 0, j)),
        out_shape=jax.ShapeDtypeStruct((depth, r, w3), F32),
        compiler_params=_cparams(("arbitrary", "arbitrary")),
        name="ada_mod",
    )(cond, ada_w, ada_b.reshape(depth, 1, w3))


def _prenorm_kernel(x_ref, g_ref, sc_ref, sh_ref, o_ref):
    x = x_ref[0]
    y = x * lax.rsqrt(jnp.mean(x * x, axis=-1, keepdims=True) + NORM_EPS)
    o_ref[0] = ((y * g_ref[...]) * (1.0 + sc_ref[0]) + sh_ref[0]).astype(BF16)


def _prenorm(x, g, sc, sh):
    b, l, d = x.shape
    tl = min(512, l)
    return pl.pallas_call(
        _prenorm_kernel,
        grid=(b, l // tl),
        in_specs=[
            pl.BlockSpec((1, tl, d), lambda i, j: (i, j, 0)),
            pl.BlockSpec((1, d), lambda i, j: (0, 0)),
            pl.BlockSpec((1, 1, d), lambda i, j: (i, 0, 0)),
            pl.BlockSpec((1, 1, d), lambda i, j: (i, 0, 0)),
        ],
        out_specs=pl.BlockSpec((1, tl, d), lambda i, j: (i, j, 0)),
        out_shape=jax.ShapeDtypeStruct((b, l, d), BF16),
        compiler_params=_cparams(("arbitrary", "arbitrary")),
        name="prenorm",
    )(x, g.reshape(1, d), sc, sh)


def _proj_kernel(*refs, epi, nsub, tm, tn, r, hw, j0):
    if epi == "rope":
        h_ref, w_ref, cos_ref, sin_ref, o_ref, wb_scr = refs
    else:
        h_ref, w_ref, o_ref, wb_scr = refs

    @pl.when(pl.program_id(1) == 0)
    def _():
        wb_scr[...] = w_ref[...].astype(BF16)

    qk_scale = jnp.where(j0 + pl.program_id(0) < r, 1.0, QK_DIM ** -0.5).astype(F32)
    cw = tn // nsub
    sw = 128 if epi == "rope" else cw
    for s in range(nsub):
        acc = _bdot(h_ref[...], wb_scr[:, s * cw:(s + 1) * cw])
        for g in range(cw // sw):
            val = acc[:, g * sw:(g + 1) * sw]
            c0 = s * cw + g * sw
            if epi == "rope":
                t = (c0 // 128 % 2) * 128
                val = (val * cos_ref[:, t:t + 128]
                       + pltpu.roll(val, 64, 1) * sin_ref[:, t:t + 128]) * qk_scale
            elif epi == "qkscale":
                val = val * qk_scale
            elif epi == "silu":
                val = _silu(val)
            elif epi == "sigmoid":
                val = _sigmoid(val)
            val = val.astype(BF16)
            if hw:
                o_ref[c0 // hw, :, c0 % hw:c0 % hw + sw] = val
            else:
                o_ref[:, c0:c0 + sw] = val


def _rope_tables(l):
    quarter = QK_DIM // 4
    inv = 1.0 / (ROPE_BASE ** (jnp.arange(quarter, dtype=F32) / quarter))
    t = jnp.arange(l)
    row = (t // GRID_W).astype(F32)
    col = (t % GRID_W).astype(F32)
    ar = row[:, None] * inv[None, :]
    ac = col[:, None] * inv[None, :]
    cos_t = jnp.concatenate([jnp.cos(ar), jnp.cos(ar), jnp.cos(ac), jnp.cos(ac)], axis=-1)
    sin_t = jnp.concatenate([-jnp.sin(ar), jnp.sin(ar), -jnp.sin(ac), jnp.sin(ac)], axis=-1)
    return cos_t, sin_t


def _proj(h, w, layer, col0, ncols, l, epi, hw=None):
    m, d = h.shape
    tm = min(1024, m)
    tn = min(1024, d)
    j0 = col0 // tn
    assert epi != "rope" or l % tm == 0
    in_specs = [
        pl.BlockSpec((tm, d), lambda j, i: (i, 0)),
        pl.BlockSpec((None, d, tn), lambda j, i: (layer, 0, j0 + j)),
    ]
    args = [h, w]
    if epi == "rope":
        lb = l // tm
        in_specs += [pl.BlockSpec((tm, QK_DIM), lambda j, i: (i % lb, 0))] * 2
        args += list(_rope_tables(l))
    if hw:
        out_spec = pl.BlockSpec((tn // hw, tm, hw), lambda j, i: (j, i, 0))
        out_shape = jax.ShapeDtypeStruct((ncols // hw, m, hw), BF16)
    else:
        out_spec = pl.BlockSpec((tm, tn), lambda j, i: (i, j))
        out_shape = jax.ShapeDtypeStruct((m, ncols), BF16)
    return pl.pallas_call(
        functools.partial(_proj_kernel, epi=epi, nsub=PROJ_NSUB, tm=tm, tn=tn, r=d // tn, hw=hw,
                          j0=j0),
        grid=(ncols // tn, m // tm),
        in_specs=in_specs,
        out_specs=out_spec,
        out_shape=out_shape,
        scratch_shapes=[pltpu.VMEM((d, tn), BF16)],
        compiler_params=_cparams(("arbitrary", "arbitrary")),
        name="in_proj_" + epi,
    )(*args)


def _hyfront_kernel(v_ref, x0_ref, x1_ref, g_ref, wv_ref, w0_ref, w1_ref,
                    bv_ref, b0_ref, b1_ref, u_ref, m_ref, *, l):
    edge = lax.broadcasted_iota(jnp.int32, (8, 1), 0)

    def conv3(x_ref, w_ref, b_ref):
        x = x_ref[0].astype(F32)
        prev = pltpu.roll(x, 1, 0)
        prev = jnp.concatenate([jnp.where(edge == 0, 0.0, prev[:8]), prev[8:]], axis=0)
        nxt = pltpu.roll(x, l - 1, 0)
        nxt = jnp.concatenate([nxt[:l - 8], jnp.where(edge == 7, 0.0, nxt[l - 8:])], axis=0)
        return prev * w_ref[0:1, :] + x * w_ref[1:2, :] + nxt * w_ref[2:3, :] + b_ref[...]

    hv = conv3(v_ref, wv_ref, bv_ref)
    hx1 = conv3(x1_ref, w1_ref, b1_ref)
    u_ref[0] = (hv * hx1).astype(BF16)
    hx0 = conv3(x0_ref, w0_ref, b0_ref)
    m_ref[0] = (hx0 * g_ref[0].astype(F32)).astype(BF16)


def _hyena_front(phy, phg, conv_w, conv_b):
    b, l, d = phg.shape
    cs = 128
    nb = d // cs
    pspec = lambda off: pl.BlockSpec((1, l, cs), lambda i, j: (i, 0, off * nb + j))
    wspec = lambda off: pl.BlockSpec((3, cs), lambda i, j: (0, off * nb + j))
    bspec = lambda off: pl.BlockSpec((1, cs), lambda i, j: (0, off * nb + j))
    ospec = pl.BlockSpec((1, l, cs), lambda i, j: (i, 0, j))
    return pl.pallas_call(
        functools.partial(_hyfront_kernel, l=l),
        grid=(b, nb),
        in_specs=[pspec(0), pspec(1), pspec(2), pspec(0),
                  wspec(0), wspec(1), wspec(2), bspec(0), bspec(1), bspec(2)],
        out_specs=[ospec, ospec],
        out_shape=[jax.ShapeDtypeStruct((b, l, d), BF16)] * 2,
        compiler_params=_cparams(("arbitrary", "arbitrary")),
        name="hyena_front",
    )(phy, phy, phy, phg, conv_w, conv_w, conv_w,
      conv_b.reshape(1, -1), conv_b.reshape(1, -1), conv_b.reshape(1, -1))


def _filter_kernel(z_ref, t_ref, w1_ref, b1_ref, w2_ref, b2_ref, w3_ref, b3_ref, fr_ref,
                   wo_ref, dl_ref, o_ref, *, tr, l):
    dot = functools.partial(jnp.dot, precision=lax.Precision.HIGHEST, preferred_element_type=F32)
    f = fr_ref[...]
    h = jnp.sin(f * (dot(z_ref[...], w1_ref[...]) + b1_ref[...]))
    h = jnp.sin(f * (dot(h, w2_ref[...]) + b2_ref[...]))
    h = jnp.sin(f * (dot(h, w3_ref[...]) + b3_ref[...]))
    y = dot(h, wo_ref[...])
    win = jnp.exp(-t_ref[...] * dl_ref[...])
    rows = pl.program_id(0) * tr + lax.broadcasted_iota(jnp.int32, (tr, 1), 0)
    o_ref[...] = jnp.where(rows == l, 0.0, y * win).astype(BF16)


def _hyena_filter_taps(l, c, w1, b1, w2, b2, w3, b3, freq, w_out):
    t = jnp.linspace(0.0, 1.0, l, dtype=F32)[:, None]
    ang = 2.0 * math.pi * jnp.arange(l, dtype=F32)[:, None] / l
    f = jnp.linspace(1e-4, HY_BANDS - 1, HY_BANDS, dtype=F32)[None, :]
    z = jnp.concatenate([t, jnp.cos(f * ang), -jnp.sin(f * ang)], axis=-1)
    z = jnp.pad(z, ((0, 0), (0, HY_EMB_PAD - HY_EMB)))
    back = lambda a: jnp.concatenate([a[l - 1:l], a[:0:-1]], axis=0)
    z2 = jnp.concatenate([z, back(z)], axis=0)
    t2 = jnp.concatenate([t, back(t)], axis=0)
    max_decay = math.log(HY_DECAY_TARGET) / HY_FAST_DECAY_PCT
    min_decay = math.log(HY_DECAY_TARGET) / HY_SLOW_DECAY_PCT
    deltas = jnp.abs(jnp.linspace(min_decay, max_decay, c, dtype=F32))[None, :]
    w1p = jnp.pad(w1, ((0, HY_EMB_PAD - HY_EMB), (0, 0)))
    hid = w2.shape[0]
    tr = min(512, l)
    lb = l // tr
    full = lambda shape: pl.BlockSpec(shape, lambda i: (0, 0))
    return pl.pallas_call(
        functools.partial(_filter_kernel, tr=tr, l=l),
        grid=(2 * lb,),
        in_specs=[
            pl.BlockSpec((tr, HY_EMB_PAD), lambda i: (i, 0)),
            pl.BlockSpec((tr, 1), lambda i: (i, 0)),
            full((HY_EMB_PAD, hid)), full((1, hid)),
            full((hid, hid)), full((1, hid)),
            full((hid, hid)), full((1, hid)),
            full((1, hid)),
            pl.BlockSpec((hid, c), lambda i: (0, i // lb)),
            full((1, c)),
        ],
        out_specs=pl.BlockSpec((tr, c), lambda i: (i, 0)),
        out_shape=jax.ShapeDtypeStruct((2 * l, c), BF16),
        compiler_params=_cparams(("arbitrary",)),
        name="hyena_filter",
    )(z2, t2, w1p, b1.reshape(1, -1), w2, b2.reshape(1, -1), w3, b3.reshape(1, -1),
      freq.reshape(1, -1), w_out, deltas)


def _dft_consts(l, n2):
    n = 2 * l
    n1 = n // n2
    nin = n1 // 2

    def cs(k, m, period):
        ph = 2.0 * np.pi * ((np.outer(k, m)) % period) / period
        return np.cos(ph), np.sin(ph)

    k1 = np.arange(n1)
    c, s = cs(k1, np.arange(nin), n1)
    f1 = np.block([[c, s], [-s, c]])
    cf, sf = cs(k1, np.arange(n1), n1)
    f1_real = np.concatenate([cf, -sf], axis=0)
    ci, si = cs(np.arange(nin), k1, n1)
    g1 = np.block([[ci, -si], [si, ci]]) / n
    out = dict(n1=n1, n2=n2, nin=nin,
               f1=jnp.asarray(f1, F32).astype(BF16),
               f1_real=jnp.asarray(f1_real, F32).astype(BF16),
               g1=jnp.asarray(g1, F32).astype(BF16))
    if n2 > 1:
        f1_il = np.empty_like(f1)
        f1_il[0::2], f1_il[1::2] = f1[:n1], f1[n1:]
        out["f1_il"] = jnp.asarray(f1_il, F32).astype(BF16)
        out["g1_il"] = jnp.asarray(f1_il.T / n, F32).astype(BF16)
        f1r_il = np.empty_like(f1_real)
        f1r_il[0::2], f1r_il[1::2] = f1_real[:n1], f1_real[n1:]
        out["f1_real_il"] = jnp.asarray(f1r_il, F32).astype(BF16)
        m2 = np.arange(n2)
        ph = (m2[None, None, :] * k1[:, None, None] + n1 * m2[None, None, :] * m2[None, :, None]) % n
        ph = 2.0 * np.pi * ph / n
        tr, ti = np.cos(ph), -np.sin(ph)
        t_fwd = np.concatenate([np.concatenate([tr, -ti], axis=2),
                                np.concatenate([ti, tr], axis=2)], axis=1)
        t_il = np.empty_like(t_fwd)
        t_il[:, :, 0::2], t_il[:, :, 1::2] = t_fwd[:, :, :n2], t_fwd[:, :, n2:]
        out["t_il"] = jnp.asarray(t_il, F32).astype(BF16)
    return out


def _cmm_kernel(*refs, mode, chunk, nchunk):
    a_ref, x_ref, o_ref = refs[0], refs[1], refs[-1]
    a = a_ref[...]
    hm = a.shape[0] // 2
    for cc in range(nchunk):
        sl = slice(cc * chunk, (cc + 1) * chunk)
        acc = _bdot(a, x_ref[0, :, sl])
        if mode == "kfmul":
            kf_ref = refs[2]
            xr, xi = acc[:hm], acc[hm:]
            kr, ki = kf_ref[0, :, sl], kf_ref[1, :, sl]
            o_ref[0, :hm, sl] = (xr * kr - xi * ki).astype(o_ref.dtype)
            o_ref[0, hm:, sl] = (xr * ki + xi * kr).astype(o_ref.dtype)
        elif mode == "epi":
            u_ref, m_ref, b_ref = refs[2], refs[3], refs[4]
            u = u_ref[0, :, sl].astype(F32)
            o = (acc + b_ref[:, sl] * u) * m_ref[0, :, sl].astype(F32)
            o_ref[0, :, sl] = o.astype(o_ref.dtype)
        else:
            o_ref[0, :, sl] = acc.astype(o_ref.dtype)


def _cmm(a, x, mode="plain", extra=(), out_dtype=BF16):
    mr, k = a.shape
    g, _, cols = x.shape
    tc = min(8192, cols)
    chunk = min(1024, tc)
    in_specs = [pl.BlockSpec((mr, k), lambda i, j: (0, 0)),
                pl.BlockSpec((1, k, tc), lambda i, j: (i, 0, j))]
    if mode == "kfmul":
        in_specs += [pl.BlockSpec((2, mr // 2, tc), lambda i, j: (0, 0, j))]
    elif mode == "epi":
        in_specs += [pl.BlockSpec((1, mr, tc), lambda i, j: (i, 0, j)),
                     pl.BlockSpec((1, mr, tc), lambda i, j: (i, 0, j)),
                     pl.BlockSpec((1, tc), lambda i, j: (0, j))]
    return pl.pallas_call(
        functools.partial(_cmm_kernel, mode=mode, chunk=chunk, nchunk=tc // chunk),
        grid=(g, cols // tc),
        in_specs=in_specs,
        out_specs=pl.BlockSpec((1, mr, tc), lambda i, j: (i, 0, j)),
        out_shape=jax.ShapeDtypeStruct((g, mr, cols), out_dtype),
        compiler_params=_cparams(("arbitrary", "arbitrary")),
        name="dft_" + mode,
    )(a, x, *extra)


def _filter_fft_kernel(x_ref, f1_ref, t_ref, o_ref, s1, ab, *, n1, n2):
    def fill(i, carry):
        src = pl.ds(pl.multiple_of(i * n2, n2), n2)
        s1[pl.ds(pl.multiple_of(i * FFT_PITCH, 8), n2), :] = x_ref[src, :].astype(F32)
        return carry

    lax.fori_loop(0, n1, fill, 0, unroll=8)

    def stage1(j, carry):
        rows = pl.ds(j, n1, stride=FFT_PITCH)
        z = s1[rows, :].astype(BF16)
        ab[rows, :] = pltpu.bitcast(_bdot(f1_ref[...], z).astype(BF16), jnp.uint32)
        return carry

    lax.fori_loop(0, n2, stage1, 0, unroll=FFT_UNROLL)

    def mid(k, carry):
        a = pltpu.bitcast(ab[pl.ds(pl.multiple_of(k * FFT_PITCH, 8), n2), :], BF16)
        x = _bdot(t_ref[k], a)
        o_ref[0, k] = x[:n2].astype(BF16)
        o_ref[1, k] = x[n2:].astype(BF16)
        return carry

    lax.fori_loop(0, n1, mid, 0, unroll=FFT_UNROLL)


def _filter_spectrum(taps, consts):
    n, c = taps.shape
    n1, n2 = consts["n1"], consts["n2"]
    if n2 == 1:
        x = taps.reshape(1, n1, c)
        return _cmm(consts["f1_real"], x, out_dtype=F32).reshape(2, n1, c)
    const = lambda shape: pl.BlockSpec(shape, lambda j: (0,) * len(shape))
    return pl.pallas_call(
        functools.partial(_filter_fft_kernel, n1=n1, n2=n2),
        grid=(c // LANES,),
        in_specs=[pl.BlockSpec((n, LANES), lambda j: (0, j)),
                  const((2 * n1, n1)), const((n1, 2 * n2, 2 * n2))],
        out_specs=pl.BlockSpec((2, n1, n2, LANES), lambda j: (0, 0, 0, j)),
        out_shape=jax.ShapeDtypeStruct((2, n1, n2, c), BF16),
        scratch_shapes=[pltpu.VMEM((n1 * FFT_PITCH, LANES), F32),
                        pltpu.VMEM((n1 * FFT_PITCH, LANES), jnp.uint32)],
        compiler_params=_cparams(("arbitrary",)),
        name="filter_fft",
    )(taps, consts["f1_real_il"], consts["t_il"])


def _conv_kernel(u_ref, m_ref, hb_ref, kf_ref, f1_ref, t_ref, g1_ref, o_ref, s1, ab, *, n1, n2, nb):
    nin = n1 // 2
    npair = nb // 2
    lanes = lambda f: jnp.concatenate([f(s) for s in range(npair)], axis=1)

    for bi in range(nb):
        def fill(i, carry, bi=bi):
            src = pl.ds(pl.multiple_of(i * n2, n2), n2)
            dst = pl.ds(pl.multiple_of(i * FFT_PITCH, 8), n2)
            s1[bi % 2, bi // 2, dst, :] = u_ref[bi, src, :].astype(F32)
            return carry
        lax.fori_loop(0, nin, fill, 0, unroll=8)

    def stage1(j, carry):
        rows = pl.ds(j, nin, stride=FFT_PITCH)
        z = jnp.concatenate([lanes(lambda s: s1[0, s, rows, :]),
                             lanes(lambda s: s1[1, s, rows, :])], axis=0).astype(BF16)
        a = pltpu.bitcast(_bdot(f1_ref[...], z).astype(BF16), jnp.uint32)
        for s in range(npair):
            ab[s, pl.ds(j, n1, stride=FFT_PITCH), :] = a[:, s * LANES:(s + 1) * LANES]
        return carry

    lax.fori_loop(0, n2, stage1, 0, unroll=FFT_UNROLL)

    def mid(k, carry):
        rows = pl.ds(pl.multiple_of(k * FFT_PITCH, 8), n2)
        a = pltpu.bitcast(lanes(lambda s: ab[s, rows, :]), BF16)
        t = t_ref[k]
        x = _bdot(t, a)
        xr, xi = x[:n2], x[n2:]
        kr = lanes(lambda s: kf_ref[0, k].astype(F32))
        ki = lanes(lambda s: kf_ref[1, k].astype(F32))
        y = jnp.concatenate([xr * kr - xi * ki, xr * ki + xi * kr], axis=0).astype(BF16)
        bm = lax.dot_general(t, y, (((0,), (0,)), ((), ())), preferred_element_type=F32)
        w = pltpu.bitcast(bm.astype(BF16), jnp.uint32)
        for s in range(npair):
            ab[s, rows, :] = w[:, s * LANES:(s + 1) * LANES]
        return carry

    lax.fori_loop(0, n1, mid, 0, unroll=FFT_UNROLL)

    def stage4(j, carry):
        b = pltpu.bitcast(lanes(lambda s: ab[s, pl.ds(j, n1, stride=FFT_PITCH), :]), BF16)
        y = _bdot(g1_ref[...], b)
        rows = pl.ds(j, nin, stride=FFT_PITCH)
        for s in range(npair):
            s1[0, s, rows, :] = y[:nin, s * LANES:(s + 1) * LANES]
            s1[1, s, rows, :] = y[nin:, s * LANES:(s + 1) * LANES]
        return carry

    lax.fori_loop(0, n2, stage4, 0, unroll=FFT_UNROLL)

    for bi in range(nb):
        def emit(i, carry, bi=bi):
            dst = pl.ds(pl.multiple_of(i * n2, n2), n2)
            src = pl.ds(pl.multiple_of(i * FFT_PITCH, 8), n2)
            y = s1[bi % 2, bi // 2, src, :] + hb_ref[...] * u_ref[bi, dst, :].astype(F32)
            o_ref[bi, dst, :] = (y * m_ref[bi, dst, :].astype(F32)).astype(BF16)
            return carry
        lax.fori_loop(0, nin, emit, 0, unroll=8)


def _long_conv(u, m, bias, kf, consts):
    b, l, c = u.shape
    n1, n2, nin = consts["n1"], consts["n2"], consts["nin"]
    p = b // 2
    if n2 == 1:
        x = u.reshape(p, 2 * nin, c)
        y = _cmm(consts["f1"], x, mode="kfmul", extra=(kf,))
        extra = (x, m.reshape(p, 2 * nin, c), bias.reshape(1, c))
        return _cmm(consts["g1"], y, mode="epi", extra=extra).reshape(b, l, c)
    const = lambda shape: pl.BlockSpec(shape, lambda j: (0,) * len(shape))
    return pl.pallas_call(
        functools.partial(_conv_kernel, n1=n1, n2=n2, nb=b),
        grid=(c // LANES,),
        in_specs=[pl.BlockSpec((b, l, LANES), lambda j: (0, 0, j)),
                  pl.BlockSpec((b, l, LANES), lambda j: (0, 0, j)),
                  pl.BlockSpec((1, LANES), lambda j: (0, j)),
                  pl.BlockSpec((2, n1, n2, LANES), lambda j: (0, 0, 0, j)),
                  const((2 * n1, n1)), const((n1, 2 * n2, 2 * n2)), const((n1, 2 * n1))],
        out_specs=pl.BlockSpec((b, l, LANES), lambda j: (0, 0, j)),
        out_shape=jax.ShapeDtypeStruct((b, l, c), BF16),
        scratch_shapes=[pltpu.VMEM((2, p, nin * FFT_PITCH, LANES), F32),
                        pltpu.VMEM((p, n1 * FFT_PITCH, LANES), jnp.uint32)],
        compiler_params=_cparams(("arbitrary",)),
        name="long_conv",
    )(u, m, bias.reshape(1, c), kf, consts["f1_il"], consts["t_il"], consts["g1_il"])


def _ret_kernel(*refs, nch, cl, states_only):
    if states_only:
        rd_ref, k_ref, v_ref, s0f_ref, s0b_ref, sf_ref, sb_ref, s_scr = refs
    else:
        (rd_ref, q_ref, k_ref, v_ref, g_ref, s0f_ref, s0b_ref,
         o_ref, sf_ref, sb_ref, s_scr, sball_scr, dmask_scr) = refs
    hd = pl.program_id(1)
    one = jnp.ones((1, 1), F32)
    lgf = -jnp.exp(one * rd_ref[0, hd])
    lgb = -jnp.exp(one * rd_ref[1, hd])
    ri = lax.broadcasted_iota(jnp.int32, (cl, 1), 0).astype(F32)
    ci = lax.broadcasted_iota(jnp.int32, (1, cl), 1).astype(F32)
    k_dec_f = jnp.exp(lgf * (cl - 1.0 - ri))
    k_dec_b = jnp.exp(lgb * ri)
    chunk_dec_f = jnp.exp(lgf * cl)
    chunk_dec_b = jnp.exp(lgb * cl)

    def kv_outer(k, dec, v):
        kd = (k.astype(F32) * dec).astype(BF16)
        return lax.dot_general(kd, v, (((0,), (0,)), ((), ())), preferred_element_type=F32)

    def load(ref, c):
        return ref[0, 0, pl.ds(pl.multiple_of(c * cl, cl), cl), :]

    s_scr[...] = s0b_ref[0, 0]

    def bstep(t, carry):
        c = nch - 1 - t
        if not states_only:
            sball_scr[c] = s_scr[...].astype(BF16)
        s_scr[...] = (s_scr[...] * chunk_dec_b
                      + kv_outer(load(k_ref, c), k_dec_b, load(v_ref, c)))
        return carry

    lax.fori_loop(0, nch, bstep, 0, unroll=RET_UNROLL)
    sb_ref[0, 0] = s_scr[...]

    s_scr[...] = s0f_ref[0, 0]
    if not states_only:
        q_dec_f = jnp.exp(lgf * (ri + 1.0))
        q_dec_b = jnp.exp(lgb * (cl - ri))
        diff = ri - ci
        dmask_scr[...] = (jnp.where(diff >= 0, jnp.exp(lgf * jnp.maximum(diff, 0.0)), 0.0)
                          + jnp.where(diff <= 0, jnp.exp(lgb * jnp.maximum(-diff, 0.0)), 0.0))

    def fstep(c, carry):
        k = load(k_ref, c)
        v = load(v_ref, c)
        if not states_only:
            q = load(q_ref, c)
            scores = lax.dot_general(q, k, (((1,), (1,)), ((), ())), preferred_element_type=F32)
            o = _bdot((scores * dmask_scr[...]).astype(BF16), v)
            qf = q.astype(F32)
            q2 = jnp.concatenate([(qf * q_dec_f).astype(BF16), (qf * q_dec_b).astype(BF16)], axis=1)
            s2 = jnp.concatenate([s_scr[...].astype(BF16), sball_scr[c]], axis=0)
            o = o + _bdot(q2, s2)
            o = o * lax.rsqrt(jnp.mean(o * o, axis=-1, keepdims=True) + NORM_EPS)
            r0 = pl.multiple_of(c * cl, cl)
            o_ref[0, pl.ds(r0, cl), :] = (o * load(g_ref, c).astype(F32)).astype(BF16)
        s_scr[...] = s_scr[...] * chunk_dec_f + kv_outer(k, k_dec_f, v)
        return carry

    lax.fori_loop(0, nch, fstep, 0, unroll=RET_UNROLL)
    sf_ref[0, 0] = s_scr[...]


def _retention(qk, k_off, v, gate, ret_decay, s0f, s0b, b, states_only=False):
    h = v.shape[0]
    l = v.shape[1] // b
    cl = min(RET_CHUNK, l)
    nch = l // cl
    sspec = pl.BlockSpec((1, 1, QK_DIM, V_DIM), lambda i, j: (i, j, 0, 0))
    state_shape = jax.ShapeDtypeStruct((b, h, QK_DIM, V_DIM), F32)
    hspec = lambda w, off: pl.BlockSpec((1, 1, l, w), lambda i, j: (off + j, i, 0, 0))
    qk4 = qk.reshape(k_off + h, b, l, QK_DIM)
    v4 = v.reshape(h, b, l, V_DIM)
    smem = pl.BlockSpec(memory_space=pltpu.SMEM)
    kern = functools.partial(_ret_kernel, nch=nch, cl=cl, states_only=states_only)
    if states_only:
        return pl.pallas_call(
            kern, grid=(b, h),
            in_specs=[smem, hspec(QK_DIM, k_off), hspec(V_DIM, 0), sspec, sspec],
            out_specs=[sspec, sspec],
            out_shape=[state_shape, state_shape],
            scratch_shapes=[pltpu.VMEM((QK_DIM, V_DIM), F32)],
            compiler_params=_cparams(("arbitrary", "arbitrary")),
            name="retention_states",
        )(ret_decay, qk4, v4, s0f, s0b)
    return pl.pallas_call(
        kern, grid=(b, h),
        in_specs=[smem, hspec(QK_DIM, 0), hspec(QK_DIM, k_off), hspec(V_DIM, 0), hspec(V_DIM, 0),
                  sspec, sspec],
        out_specs=[pl.BlockSpec((1, l, V_DIM), lambda i, j: (i, 0, j)), sspec, sspec],
        out_shape=[jax.ShapeDtypeStruct((b, l, h * V_DIM), BF16), state_shape, state_shape],
        scratch_shapes=[pltpu.VMEM((QK_DIM, V_DIM), F32),
                        pltpu.VMEM((nch, QK_DIM, V_DIM), BF16),
                        pltpu.VMEM((cl, cl), F32)],
        compiler_params=_cparams(("arbitrary", "arbitrary")),
        name="retention",
    )(ret_decay, qk4, qk4, v4, gate.reshape(h, b, l, V_DIM), s0f, s0b)


def _merge_kernel(hy_ref, ret_ref, why_ref, wret_ref, ghy_ref, gret_ref, o_ref):
    a = _bdot(hy_ref[...], why_ref[...])
    b = _bdot(ret_ref[...], wret_ref[...])
    o_ref[...] = (ghy_ref[...].astype(F32) * a + gret_ref[...].astype(F32) * b).astype(BF16)


def _merge(hy, ret, pmg, w_hy, w_ret):
    mm, d = hy.shape
    tm = min(512, mm)
    tn = min(1024, d)
    nb = d // tn
    return pl.pallas_call(
        _merge_kernel,
        grid=(mm // tm, nb),
        in_specs=[
            pl.BlockSpec((tm, d), lambda i, j: (i, 0)),
            pl.BlockSpec((tm, 2 * d), lambda i, j: (i, 0)),
            pl.BlockSpec((d, tn), lambda i, j: (0, j)),
            pl.BlockSpec((2 * d, tn), lambda i, j: (0, j)),
            pl.BlockSpec((tm, tn), lambda i, j: (i, j)),
            pl.BlockSpec((tm, tn), lambda i, j: (i, nb + j)),
        ],
        out_specs=pl.BlockSpec((tm, tn), lambda i, j: (i, j)),
        out_shape=jax.ShapeDtypeStruct((mm, d), BF16),
        compiler_params=_cparams(("arbitrary", "arbitrary")),
        name="merge_proj",
    )(hy, ret, w_hy, w_ret, pmg, pmg)


def _wo_kernel(*refs, final):
    if final:
        m_ref, w_ref, x_ref, g_ref, fg_ref, o_ref = refs
    else:
        m_ref, w_ref, x_ref, g_ref, o_ref = refs
    xn = x_ref[0] + g_ref[0] * _bdot(m_ref[0], w_ref[...])
    if final:
        xn = xn * lax.rsqrt(jnp.mean(xn * xn, axis=-1, keepdims=True) + NORM_EPS) * fg_ref[...]
    o_ref[0] = xn


def _wo_residual(mrg, w_o, x, gate, final_g=None):
    b, l, d = x.shape
    tl = min(512, l)
    final = final_g is not None
    in_specs = [
        pl.BlockSpec((1, tl, d), lambda i, j: (i, j, 0)),
        pl.BlockSpec((d, d), lambda i, j: (0, 0)),
        pl.BlockSpec((1, tl, d), lambda i, j: (i, j, 0)),
        pl.BlockSpec((1, 1, d), lambda i, j: (i, 0, 0)),
    ]
    args = [mrg.reshape(b, l, d), w_o, x, gate]
    if final:
        in_specs.append(pl.BlockSpec((1, d), lambda i, j: (0, 0)))
        args.append(final_g.reshape(1, d))
    return pl.pallas_call(
        functools.partial(_wo_kernel, final=final),
        grid=(b, l // tl),
        in_specs=in_specs,
        out_specs=pl.BlockSpec((1, tl, d), lambda i, j: (i, j, 0)),
        out_shape=jax.ShapeDtypeStruct((b, l, d), F32),
        compiler_params=_cparams(("arbitrary", "arbitrary")),
        name="wo_residual",
    )(*args)


def _mixer(h, use_rope, s0f, s0b, w_in, layer, conv_w, conv_b, kf, consts, hy_bias, ret_decay,
           w_hy_b, w_ret_b):
    b, l, d = h.shape
    h2 = h.reshape(b * l, d)
    proj = functools.partial(_proj, h2, w_in, layer)
    qk = proj(0, 2 * d, l, "rope" if use_rope else "qkscale", hw=QK_DIM)
    v = proj(2 * d, 2 * d, l, "plain", hw=V_DIM)
    rg = proj(4 * d, 2 * d, l, "silu", hw=V_DIM)
    phy = proj(6 * d, 3 * d, l, "plain")
    phg = proj(9 * d, d, l, "silu")
    pmg = proj(10 * d, 2 * d, l, "sigmoid")
    u, m = _hyena_front(phy.reshape(b, l, 3 * d), phg.reshape(b, l, d), conv_w, conv_b)
    hy = _long_conv(u, m, hy_bias, kf, consts)
    ret, sf, sb = _retention(qk, d // QK_DIM, v, rg, ret_decay, s0f, s0b, b)
    mrg = _merge(hy.reshape(b * l, d), ret.reshape(b * l, 2 * d), pmg, w_hy_b, w_ret_b)
    return mrg, sf, sb


def kernel(x, c, ctx, c_ctx, ln_g, ada_w, ada_b, w_in, hy_conv_w, hy_conv_b, hy_filt_w1,
           hy_filt_b1, hy_filt_w2, hy_filt_b2, hy_filt_w3, hy_filt_b3, hy_filt_freq,
           hy_filt_wout, hy_bias, ret_decay, w_hy_out, w_ret_out, w_o, final_g):
    b, l, d = x.shape
    lc = ctx.shape[1]
    depth = ln_g.shape[0]
    h = d // QK_DIM
    assert b % 2 == 0 and d % QK_DIM == 0 and l % GRID_W == 0

    rows = -(-(b + 1) // 8) * 8
    cond = jnp.zeros((rows, d), F32).at[:b].set(c).at[b].set(c_ctx)
    mod = _ada_modulation(cond, ada_w, ada_b)

    consts_l = _dft_consts(l, FFT_N2 if (2 * l) % FFT_N2 == 0 and l >= 1024 else 1)
    consts_c = _dft_consts(lc, 1)
    zero_state = jnp.zeros((b, h, QK_DIM, V_DIM), F32)

    for i in range(depth):
        sh, sc, gt = mod[i, :, :d], mod[i, :, d:2 * d], mod[i, :, 2 * d:]
        lat = lambda a: a[:b, None, :]
        cx = lambda a: jnp.broadcast_to(a[b][None, None, :], (b, 1, d))
        w_hy_b = w_hy_out[i].astype(BF16)
        w_ret_b = w_ret_out[i].astype(BF16)
        w_o_b = w_o[i].astype(BF16)
        filt = (hy_filt_w1[i], hy_filt_b1[i], hy_filt_w2[i], hy_filt_b2[i], hy_filt_w3[i],
                hy_filt_b3[i], hy_filt_freq[i], hy_filt_wout[i])
        params = lambda kf, consts: (w_in, i, hy_conv_w[i], hy_conv_b[i], kf, consts, hy_bias[i],
                                     ret_decay[i], w_hy_b, w_ret_b)

        h_ctx = _prenorm(ctx, ln_g[i], cx(sc), cx(sh))
        if i < depth - 1:
            kf_c = _filter_spectrum(_hyena_filter_taps(lc, d, *filt), consts_c)
            mrg_c, s_ctx_f, s_ctx_b = _mixer(h_ctx, False, zero_state, zero_state,
                                             *params(kf_c, consts_c))
            ctx_next = _wo_residual(mrg_c, w_o_b, ctx, cx(gt))
        else:
            hc2 = h_ctx.reshape(b * lc, d)
            k_c = _proj(hc2, w_in, i, d, d, lc, "qkscale", hw=QK_DIM)
            v_c = _proj(hc2, w_in, i, 2 * d, 2 * d, lc, "plain", hw=V_DIM)
            s_ctx_f, s_ctx_b = _retention(k_c, 0, v_c, None, ret_decay[i], zero_state, zero_state,
                                          b, states_only=True)
            ctx_next = ctx

        h_lat = _prenorm(x, ln_g[i], lat(sc), lat(sh))
        kf_l = _filter_spectrum(_hyena_filter_taps(l, d, *filt), consts_l)
        mrg, _, _ = _mixer(h_lat, True, s_ctx_f, s_ctx_b, *params(kf_l, consts_l))
        x = _wo_residual(mrg, w_o_b, x, lat(gt), final_g if i == depth - 1 else None)
        ctx = ctx_next

    return x
```

```python
import functools
import math

import numpy as np
import jax
import jax.numpy as jnp
from jax import lax
from jax.experimental import pallas as pl
from jax.experimental.pallas import tpu as pltpu

F32 = jnp.float32
BF16 = jnp.bfloat16

NORM_EPS = 1e-6
GRID_W = 64
ROPE_BASE = 10000.0
QK_DIM = 256
V_DIM = 512
RET_CHUNK = 256
RET_UNROLL = 2
HY_EMB = 33
HY_BANDS = (HY_EMB - 1) // 2
HY_EMB_PAD = 64
HY_FAST_DECAY_PCT = 0.3
HY_SLOW_DECAY_PCT = 1.5
HY_DECAY_TARGET = 1e-2
FFT_N2 = 64
PROJ_NSUB = 4
LANES = 128
FFT_PITCH = 72
FFT_UNROLL = 16
VMEM_LIMIT = 56 * 1024 * 1024


def _cparams(sem):
    return pltpu.CompilerParams(dimension_semantics=sem, vmem_limit_bytes=VMEM_LIMIT)


def _silu(x):
    return x / (1.0 + jnp.exp(-x))


def _sigmoid(x):
    return 1.0 / (1.0 + jnp.exp(-x))


def _bdot(a, b):
    return jnp.dot(a, b, preferred_element_type=F32)


def _ada_kernel(c_ref, w_ref, b_ref, o_ref):
    s = _silu(c_ref[...]).astype(BF16)
    o_ref[0] = _bdot(s, w_ref[0].astype(BF16)) + b_ref[0]


def _ada_modulation(cond, ada_w, ada_b):
    depth, d, w3 = ada_w.shape
    r = cond.shape[0]
    tn = min(512, w3)
    return pl.pallas_call(
        _ada_kernel,
        grid=(depth, w3 // tn),
        in_specs=[
            pl.BlockSpec((r, d), lambda l, j: (0, 0)),
            pl.BlockSpec((1, d, tn), lambda l, j: (l, 0, j)),
            pl.BlockSpec((1, 1, tn), lambda l, j: (l, 0, j)),
        ],
        out_specs=pl.BlockSpec((1, r, tn), lambda l, j: (l, 0, j)),
        out_shape=jax.ShapeDtypeStruct((depth, r, w3), F32),
        compiler_params=_cparams(("arbitrary", "arbitrary")),
        name="ada_mod",
    )(cond, ada_w, ada_b.reshape(depth, 1, w3))


def _prenorm_kernel(x_ref, g_ref, sc_ref, sh_ref, o_ref):
    x = x_ref[0]
    y = x * lax.rsqrt(jnp.mean(x * x, axis=-1, keepdims=True) + NORM_EPS)
    o_ref[0] = ((y * g_ref[...]) * (1.0 + sc_ref[0]) + sh_ref[0]).astype(BF16)


def _prenorm(x, g, sc, sh):
    b, l, d = x.shape
    tl = min(512, l)
    return pl.pallas_call(
        _prenorm_kernel,
        grid=(b, l // tl),
        in_specs=[
            pl.BlockSpec((1, tl, d), lambda i, j: (i, j, 0)),
            pl.BlockSpec((1, d), lambda i, j: (0, 0)),
            pl.BlockSpec((1, 1, d), lambda i, j: (i, 0, 0)),
            pl.BlockSpec((1, 1, d), lambda i, j: (i, 0, 0)),
        ],
        out_specs=pl.BlockSpec((1, tl, d), lambda i, j: (i, j, 0)),
        out_shape=jax.ShapeDtypeStruct((b, l, d), BF16),
        compiler_params=_cparams(("arbitrary", "arbitrary")),
        name="prenorm",
    )(x, g.reshape(1, d), sc, sh)


def _proj_kernel(*refs, epi, nsub, tm, tn, r, hw, j0):
    if epi == "rope":
        h_ref, w_ref, cos_ref, sin_ref, o_ref, wb_scr = refs
    else:
        h_ref, w_ref, o_ref, wb_scr = refs

    @pl.when(pl.program_id(1) == 0)
    def _():
        wb_scr[...] = w_ref[...].astype(BF16)

    qk_scale = jnp.where(j0 + pl.program_id(0) < r, 1.0, QK_DIM ** -0.5).astype(F32)
    ts = tm // nsub
    sw = 128 if epi == "rope" else (hw or tn)
    for s in range(nsub):
        rows = slice(s * ts, (s + 1) * ts)
        acc = _bdot(h_ref[rows, :], wb_scr[...])
        for g in range(tn // sw):
            val = acc[:, g * sw:(g + 1) * sw]
            if epi == "rope":
                t = (g % 2) * 128
                val = (val * cos_ref[rows, t:t + 128]
                       + pltpu.roll(val, 64, 1) * sin_ref[rows, t:t + 128]) * qk_scale
            elif epi == "qkscale":
                val = val * qk_scale
            elif epi == "silu":
                val = _silu(val)
            elif epi == "sigmoid":
                val = _sigmoid(val)
            val = val.astype(BF16)
            if hw:
                c0 = g * sw
                o_ref[c0 // hw, rows, c0 % hw:c0 % hw + sw] = val
            else:
                o_ref[rows, g * sw:(g + 1) * sw] = val


def _rope_tables(l):
    quarter = QK_DIM // 4
    inv = 1.0 / (ROPE_BASE ** (jnp.arange(quarter, dtype=F32) / quarter))
    t = jnp.arange(l)
    row = (t // GRID_W).astype(F32)
    col = (t % GRID_W).astype(F32)
    ar = row[:, None] * inv[None, :]
    ac = col[:, None] * inv[None, :]
    cos_t = jnp.concatenate([jnp.cos(ar), jnp.cos(ar), jnp.cos(ac), jnp.cos(ac)], axis=-1)
    sin_t = jnp.concatenate([-jnp.sin(ar), jnp.sin(ar), -jnp.sin(ac), jnp.sin(ac)], axis=-1)
    return cos_t, sin_t


def _proj(h, w, layer, col0, ncols, l, epi, hw=None):
    m, d = h.shape
    tm = min(1024, m)
    tn = min(1024, d)
    j0 = col0 // tn
    assert epi != "rope" or l % tm == 0
    in_specs = [
        pl.BlockSpec((tm, d), lambda j, i: (i, 0)),
        pl.BlockSpec((None, d, tn), lambda j, i: (layer, 0, j0 + j)),
    ]
    args = [h, w]
    if epi == "rope":
        lb = l // tm
        in_specs += [pl.BlockSpec((tm, QK_DIM), lambda j, i: (i % lb, 0))] * 2
        args += list(_rope_tables(l))
    if hw:
        out_spec = pl.BlockSpec((tn // hw, tm, hw), lambda j, i: (j, i, 0))
        out_shape = jax.ShapeDtypeStruct((ncols // hw, m, hw), BF16)
    else:
        out_spec = pl.BlockSpec((tm, tn), lambda j, i: (i, j))
        out_shape = jax.ShapeDtypeStruct((m, ncols), BF16)
    return pl.pallas_call(
        functools.partial(_proj_kernel, epi=epi, nsub=PROJ_NSUB, tm=tm, tn=tn, r=d // tn, hw=hw,
                          j0=j0),
        grid=(ncols // tn, m // tm),
        in_specs=in_specs,
        out_specs=out_spec,
        out_shape=out_shape,
        scratch_shapes=[pltpu.VMEM((d, tn), BF16)],
        compiler_params=_cparams(("arbitrary", "arbitrary")),
        name="in_proj_" + epi,
    )(*args)


def _hyfront_kernel(v_ref, x0_ref, x1_ref, g_ref, wv_ref, w0_ref, w1_ref,
                    bv_ref, b0_ref, b1_ref, u_ref, m_ref, *, l):
    rows = lax.broadcasted_iota(jnp.int32, (l, 1), 0)

    def conv3(x_ref, w_ref, b_ref):
        x = x_ref[0].astype(F32)
        prev = jnp.where(rows == 0, 0.0, pltpu.roll(x, 1, 0))
        nxt = jnp.where(rows == l - 1, 0.0, pltpu.roll(x, l - 1, 0))
        return prev * w_ref[0:1, :] + x * w_ref[1:2, :] + nxt * w_ref[2:3, :] + b_ref[...]

    hv = conv3(v_ref, wv_ref, bv_ref)
    hx1 = conv3(x1_ref, w1_ref, b1_ref)
    u_ref[0] = (hv * hx1).astype(BF16)
    hx0 = conv3(x0_ref, w0_ref, b0_ref)
    m_ref[0] = (hx0 * g_ref[0].astype(F32)).astype(BF16)


def _hyena_front(phy, phg, conv_w, conv_b):
    b, l, d = phg.shape
    cs = 128
    nb = d // cs
    pspec = lambda off: pl.BlockSpec((1, l, cs), lambda i, j: (i, 0, off * nb + j))
    wspec = lambda off: pl.BlockSpec((3, cs), lambda i, j: (0, off * nb + j))
    bspec = lambda off: pl.BlockSpec((1, cs), lambda i, j: (0, off * nb + j))
    ospec = pl.BlockSpec((1, l, cs), lambda i, j: (i, 0, j))
    return pl.pallas_call(
        functools.partial(_hyfront_kernel, l=l),
        grid=(b, nb),
        in_specs=[pspec(0), pspec(1), pspec(2), pspec(0),
                  wspec(0), wspec(1), wspec(2), bspec(0), bspec(1), bspec(2)],
        out_specs=[ospec, ospec],
        out_shape=[jax.ShapeDtypeStruct((b, l, d), BF16)] * 2,
        compiler_params=_cparams(("arbitrary", "arbitrary")),
        name="hyena_front",
    )(phy, phy, phy, phg, conv_w, conv_w, conv_w,
      conv_b.reshape(1, -1), conv_b.reshape(1, -1), conv_b.reshape(1, -1))


def _filter_kernel(z_ref, t_ref, w1_ref, b1_ref, w2_ref, b2_ref, w3_ref, b3_ref, fr_ref,
                   wo_ref, dl_ref, o_ref, *, tr, l):
    dot = functools.partial(jnp.dot, precision=lax.Precision.HIGHEST, preferred_element_type=F32)
    f = fr_ref[...]
    h = jnp.sin(f * (dot(z_ref[...], w1_ref[...]) + b1_ref[...]))
    h = jnp.sin(f * (dot(h, w2_ref[...]) + b2_ref[...]))
    h = jnp.sin(f * (dot(h, w3_ref[...]) + b3_ref[...]))
    y = _bdot(h.astype(BF16), wo_ref[...].astype(BF16))
    win = jnp.exp(-t_ref[...] * dl_ref[...])
    rows = pl.program_id(0) * tr + lax.broadcasted_iota(jnp.int32, (tr, 1), 0)
    o_ref[...] = jnp.where(rows == l, 0.0, y * win).astype(BF16)


def _hyena_filter_taps(l, c, w1, b1, w2, b2, w3, b3, freq, w_out):
    t = jnp.linspace(0.0, 1.0, l, dtype=F32)[:, None]
    ang = 2.0 * math.pi * jnp.arange(l, dtype=F32)[:, None] / l
    f = jnp.linspace(1e-4, HY_BANDS - 1, HY_BANDS, dtype=F32)[None, :]
    z = jnp.concatenate([t, jnp.cos(f * ang), -jnp.sin(f * ang)], axis=-1)
    z = jnp.pad(z, ((0, 0), (0, HY_EMB_PAD - HY_EMB)))
    back = lambda a: jnp.concatenate([a[l - 1:l], a[:0:-1]], axis=0)
    z2 = jnp.concatenate([z, back(z)], axis=0)
    t2 = jnp.concatenate([t, back(t)], axis=0)
    max_decay = math.log(HY_DECAY_TARGET) / HY_FAST_DECAY_PCT
    min_decay = math.log(HY_DECAY_TARGET) / HY_SLOW_DECAY_PCT
    deltas = jnp.abs(jnp.linspace(min_decay, max_decay, c, dtype=F32))[None, :]
    w1p = jnp.pad(w1, ((0, HY_EMB_PAD - HY_EMB), (0, 0)))
    hid = w2.shape[0]
    tr = min(512, l)
    lb = l // tr
    full = lambda shape: pl.BlockSpec(shape, lambda i: (0, 0))
    return pl.pallas_call(
        functools.partial(_filter_kernel, tr=tr, l=l),
        grid=(2 * lb,),
        in_specs=[
            pl.BlockSpec((tr, HY_EMB_PAD), lambda i: (i, 0)),
            pl.BlockSpec((tr, 1), lambda i: (i, 0)),
            full((HY_EMB_PAD, hid)), full((1, hid)),
            full((hid, hid)), full((1, hid)),
            full((hid, hid)), full((1, hid)),
            full((1, hid)),
            pl.BlockSpec((hid, c), lambda i: (0, i // lb)),
            full((1, c)),
        ],
        out_specs=pl.BlockSpec((tr, c), lambda i: (i, 0)),
        out_shape=jax.ShapeDtypeStruct((2 * l, c), BF16),
        compiler_params=_cparams(("arbitrary",)),
        name="hyena_filter",
    )(z2, t2, w1p, b1.reshape(1, -1), w2, b2.reshape(1, -1), w3, b3.reshape(1, -1),
      freq.reshape(1, -1), w_out, deltas)


def _dft_consts(l, n2):
    n = 2 * l
    n1 = n // n2
    nin = n1 // 2

    def cs(k, m, period):
        ph = 2.0 * np.pi * ((np.outer(k, m)) % period) / period
        return np.cos(ph), np.sin(ph)

    k1 = np.arange(n1)
    c, s = cs(k1, np.arange(nin), n1)
    f1 = np.block([[c, s], [-s, c]])
    cf, sf = cs(k1, np.arange(n1), n1)
    f1_real = np.concatenate([cf, -sf], axis=0)
    ci, si = cs(np.arange(nin), k1, n1)
    g1 = np.block([[ci, -si], [si, ci]]) / n
    out = dict(n1=n1, n2=n2, nin=nin,
               f1=jnp.asarray(f1, F32).astype(BF16),
               f1_real=jnp.asarray(f1_real, F32).astype(BF16),
               g1=jnp.asarray(g1, F32).astype(BF16))
    if n2 > 1:
        f1_il = np.empty_like(f1)
        f1_il[0::2], f1_il[1::2] = f1[:n1], f1[n1:]
        out["f1_il"] = jnp.asarray(f1_il, F32).astype(BF16)
        out["g1_il"] = jnp.asarray(f1_il.T / n, F32).astype(BF16)
        f1r_il = np.empty_like(f1_real)
        f1r_il[0::2], f1r_il[1::2] = f1_real[:n1], f1_real[n1:]
        out["f1_real_il"] = jnp.asarray(f1r_il, F32).astype(BF16)
        m2 = np.arange(n2)
        ph = (m2[None, None, :] * k1[:, None, None] + n1 * m2[None, None, :] * m2[None, :, None]) % n
        ph = 2.0 * np.pi * ph / n
        tr, ti = np.cos(ph), -np.sin(ph)
        t_fwd = np.concatenate([np.concatenate([tr, -ti], axis=2),
                                np.concatenate([ti, tr], axis=2)], axis=1)
        t_il = np.empty_like(t_fwd)
        t_il[:, :, 0::2], t_il[:, :, 1::2] = t_fwd[:, :, :n2], t_fwd[:, :, n2:]
        out["t_il"] = jnp.asarray(t_il, F32).astype(BF16)
    return out


def _cmm_kernel(*refs, mode, chunk, nchunk):
    a_ref, x_ref, o_ref = refs[0], refs[1], refs[-1]
    a = a_ref[...]
    hm = a.shape[0] // 2
    for cc in range(nchunk):
        sl = slice(cc * chunk, (cc + 1) * chunk)
        acc = _bdot(a, x_ref[0, :, sl])
        if mode == "kfmul":
            kf_ref = refs[2]
            xr, xi = acc[:hm], acc[hm:]
            kr, ki = kf_ref[0, :, sl], kf_ref[1, :, sl]
            o_ref[0, :hm, sl] = (xr * kr - xi * ki).astype(o_ref.dtype)
            o_ref[0, hm:, sl] = (xr * ki + xi * kr).astype(o_ref.dtype)
        elif mode == "epi":
            u_ref, m_ref, b_ref = refs[2], refs[3], refs[4]
            u = u_ref[0, :, sl].astype(F32)
            o = (acc + b_ref[:, sl] * u) * m_ref[0, :, sl].astype(F32)
            o_ref[0, :, sl] = o.astype(o_ref.dtype)
        else:
            o_ref[0, :, sl] = acc.astype(o_ref.dtype)


def _cmm(a, x, mode="plain", extra=(), out_dtype=BF16):
    mr, k = a.shape
    g, _, cols = x.shape
    tc = min(8192, cols)
    chunk = min(1024, tc)
    in_specs = [pl.BlockSpec((mr, k), lambda i, j: (0, 0)),
                pl.BlockSpec((1, k, tc), lambda i, j: (i, 0, j))]
    if mode == "kfmul":
        in_specs += [pl.BlockSpec((2, mr // 2, tc), lambda i, j: (0, 0, j))]
    elif mode == "epi":
        in_specs += [pl.BlockSpec((1, mr, tc), lambda i, j: (i, 0, j)),
                     pl.BlockSpec((1, mr, tc), lambda i, j: (i, 0, j)),
                     pl.BlockSpec((1, tc), lambda i, j: (0, j))]
    return pl.pallas_call(
        functools.partial(_cmm_kernel, mode=mode, chunk=chunk, nchunk=tc // chunk),
        grid=(g, cols // tc),
        in_specs=in_specs,
        out_specs=pl.BlockSpec((1, mr, tc), lambda i, j: (i, 0, j)),
        out_shape=jax.ShapeDtypeStruct((g, mr, cols), out_dtype),
        compiler_params=_cparams(("arbitrary", "arbitrary")),
        name="dft_" + mode,
    )(a, x, *extra)


def _filter_fft_kernel(x_ref, f1_ref, t_ref, o_ref, s1, ab, *, n1, n2):
    def fill(i, carry):
        src = pl.ds(pl.multiple_of(i * n2, n2), n2)
        s1[pl.ds(pl.multiple_of(i * FFT_PITCH, 8), n2), :] = x_ref[src, :].astype(F32)
        return carry

    lax.fori_loop(0, n1, fill, 0, unroll=8)

    def stage1(j, carry):
        rows = pl.ds(j, n1, stride=FFT_PITCH)
        z = s1[rows, :].astype(BF16)
        ab[rows, :] = pltpu.bitcast(_bdot(f1_ref[...], z).astype(BF16), jnp.uint32)
        return carry

    lax.fori_loop(0, n2, stage1, 0, unroll=FFT_UNROLL)

    def mid(k, carry):
        a = pltpu.bitcast(ab[pl.ds(pl.multiple_of(k * FFT_PITCH, 8), n2), :], BF16)
        x = _bdot(t_ref[k], a)
        o_ref[0, k] = x[:n2].astype(BF16)
        o_ref[1, k] = x[n2:].astype(BF16)
        return carry

    lax.fori_loop(0, n1, mid, 0, unroll=FFT_UNROLL)


def _filter_spectrum(taps, consts):
    n, c = taps.shape
    n1, n2 = consts["n1"], consts["n2"]
    if n2 == 1:
        x = taps.reshape(1, n1, c)
        return _cmm(consts["f1_real"], x, out_dtype=F32).reshape(2, n1, c)
    const = lambda shape: pl.BlockSpec(shape, lambda j: (0,) * len(shape))
    return pl.pallas_call(
        functools.partial(_filter_fft_kernel, n1=n1, n2=n2),
        grid=(c // LANES,),
        in_specs=[pl.BlockSpec((n, LANES), lambda j: (0, j)),
                  const((2 * n1, n1)), const((n1, 2 * n2, 2 * n2))],
        out_specs=pl.BlockSpec((2, n1, n2, LANES), lambda j: (0, 0, 0, j)),
        out_shape=jax.ShapeDtypeStruct((2, n1, n2, c), BF16),
        scratch_shapes=[pltpu.VMEM((n1 * FFT_PITCH, LANES), F32),
                        pltpu.VMEM((n1 * FFT_PITCH, LANES), jnp.uint32)],
        compiler_params=_cparams(("arbitrary",)),
        name="filter_fft",
    )(taps, consts["f1_real_il"], consts["t_il"])


def _conv_kernel(u_ref, m_ref, hb_ref, kf_ref, f1_ref, t_ref, g1_ref, o_ref, s1, ab, *, n1, n2, nb):
    nin = n1 // 2
    npair = nb // 2
    lanes = lambda f: jnp.concatenate([f(s) for s in range(npair)], axis=1)

    for bi in range(nb):
        def fill(i, carry, bi=bi):
            src = pl.ds(pl.multiple_of(i * n2, n2), n2)
            dst = pl.ds(pl.multiple_of(i * FFT_PITCH, 8), n2)
            s1[bi % 2, bi // 2, dst, :] = u_ref[bi, src, :].astype(F32)
            return carry
        lax.fori_loop(0, nin, fill, 0, unroll=8)

    def stage1(j, carry):
        rows = pl.ds(j, nin, stride=FFT_PITCH)
        z = jnp.concatenate([lanes(lambda s: s1[0, s, rows, :]),
                             lanes(lambda s: s1[1, s, rows, :])], axis=0).astype(BF16)
        a = pltpu.bitcast(_bdot(f1_ref[...], z).astype(BF16), jnp.uint32)
        for s in range(npair):
            ab[s, pl.ds(j, n1, stride=FFT_PITCH), :] = a[:, s * LANES:(s + 1) * LANES]
        return carry

    lax.fori_loop(0, n2, stage1, 0, unroll=FFT_UNROLL)

    def mid(k, carry):
        rows = pl.ds(pl.multiple_of(k * FFT_PITCH, 8), n2)
        a = pltpu.bitcast(lanes(lambda s: ab[s, rows, :]), BF16)
        t = t_ref[k]
        x = _bdot(t, a)
        xr, xi = x[:n2], x[n2:]
        kr = lanes(lambda s: kf_ref[0, k].astype(F32))
        ki = lanes(lambda s: kf_ref[1, k].astype(F32))
        y = jnp.concatenate([xr * kr - xi * ki, xr * ki + xi * kr], axis=0).astype(BF16)
        bm = lax.dot_general(t, y, (((0,), (0,)), ((), ())), preferred_element_type=F32)
        w = pltpu.bitcast(bm.astype(BF16), jnp.uint32)
        for s in range(npair):
            ab[s, rows, :] = w[:, s * LANES:(s + 1) * LANES]
        return carry

    lax.fori_loop(0, n1, mid, 0, unroll=FFT_UNROLL)

    def stage4(j, carry):
        b = pltpu.bitcast(lanes(lambda s: ab[s, pl.ds(j, n1, stride=FFT_PITCH), :]), BF16)
        y = _bdot(g1_ref[...], b)
        rows = pl.ds(j, nin, stride=FFT_PITCH)
        for s in range(npair):
            s1[0, s, rows, :] = y[:nin, s * LANES:(s + 1) * LANES]
            s1[1, s, rows, :] = y[nin:, s * LANES:(s + 1) * LANES]
        return carry

    lax.fori_loop(0, n2, stage4, 0, unroll=FFT_UNROLL)

    for bi in range(nb):
        def emit(i, carry, bi=bi):
            dst = pl.ds(pl.multiple_of(i * n2, n2), n2)
            src = pl.ds(pl.multiple_of(i * FFT_PITCH, 8), n2)
            y = s1[bi % 2, bi // 2, src, :] + hb_ref[...] * u_ref[bi, dst, :].astype(F32)
            o_ref[bi, dst, :] = (y * m_ref[bi, dst, :].astype(F32)).astype(BF16)
            return carry
        lax.fori_loop(0, nin, emit, 0, unroll=8)


def _long_conv(u, m, bias, kf, consts):
    b, l, c = u.shape
    n1, n2, nin = consts["n1"], consts["n2"], consts["nin"]
    p = b // 2
    if n2 == 1:
        x = u.reshape(p, 2 * nin, c)
        y = _cmm(consts["f1"], x, mode="kfmul", extra=(kf,))
        extra = (x, m.reshape(p, 2 * nin, c), bias.reshape(1, c))
        return _cmm(consts["g1"], y, mode="epi", extra=extra).reshape(b, l, c)
    const = lambda shape: pl.BlockSpec(shape, lambda j: (0,) * len(shape))
    return pl.pallas_call(
        functools.partial(_conv_kernel, n1=n1, n2=n2, nb=b),
        grid=(c // LANES,),
        in_specs=[pl.BlockSpec((b, l, LANES), lambda j: (0, 0, j)),
                  pl.BlockSpec((b, l, LANES), lambda j: (0, 0, j)),
                  pl.BlockSpec((1, LANES), lambda j: (0, j)),
                  pl.BlockSpec((2, n1, n2, LANES), lambda j: (0, 0, 0, j)),
                  const((2 * n1, n1)), const((n1, 2 * n2, 2 * n2)), const((n1, 2 * n1))],
        out_specs=pl.BlockSpec((b, l, LANES), lambda j: (0, 0, j)),
        out_shape=jax.ShapeDtypeStruct((b, l, c), BF16),
        scratch_shapes=[pltpu.VMEM((2, p, nin * FFT_PITCH, LANES), F32),
                        pltpu.VMEM((p, n1 * FFT_PITCH, LANES), jnp.uint32)],
        compiler_params=_cparams(("arbitrary",)),
        name="long_conv",
    )(u, m, bias.reshape(1, c), kf, consts["f1_il"], consts["t_il"], consts["g1_il"])


def _ret_kernel(*refs, nch, cl, states_only):
    if states_only:
        rd_ref, k_ref, v_ref, s0f_ref, s0b_ref, sf_ref, sb_ref, s_scr = refs
    else:
        (rd_ref, q_ref, k_ref, v_ref, g_ref, s0f_ref, s0b_ref,
         o_ref, sf_ref, sb_ref, s_scr, sball_scr, dmask_scr) = refs
    hd = pl.program_id(1)
    one = jnp.ones((1, 1), F32)
    lgf = -jnp.exp(one * rd_ref[0, hd])
    lgb = -jnp.exp(one * rd_ref[1, hd])
    ri = lax.broadcasted_iota(jnp.int32, (cl, 1), 0).astype(F32)
    ci = lax.broadcasted_iota(jnp.int32, (1, cl), 1).astype(F32)
    k_dec_f = jnp.exp(lgf * (cl - 1.0 - ri))
    k_dec_b = jnp.exp(lgb * ri)
    chunk_dec_f = jnp.exp(lgf * cl)
    chunk_dec_b = jnp.exp(lgb * cl)

    def kv_outer(k, dec, v):
        kd = (k.astype(F32) * dec).astype(BF16)
        return lax.dot_general(kd, v, (((0,), (0,)), ((), ())), preferred_element_type=F32)

    def load(ref, c):
        return ref[0, 0, pl.ds(pl.multiple_of(c * cl, cl), cl), :]

    s_scr[...] = s0b_ref[0, 0]

    def bstep(t, carry):
        c = nch - 1 - t
        if not states_only:
            sball_scr[c] = s_scr[...].astype(BF16)
        s_scr[...] = (s_scr[...] * chunk_dec_b
                      + kv_outer(load(k_ref, c), k_dec_b, load(v_ref, c)))
        return carry

    lax.fori_loop(0, nch, bstep, 0, unroll=2 * RET_UNROLL)
    sb_ref[0, 0] = s_scr[...]

    s_scr[...] = s0f_ref[0, 0]
    if not states_only:
        q_dec_f = jnp.exp(lgf * (ri + 1.0))
        q_dec_b = jnp.exp(lgb * (cl - ri))
        diff = ri - ci
        dmask_scr[...] = (jnp.where(diff >= 0, jnp.exp(lgf * jnp.maximum(diff, 0.0)), 0.0)
                          + jnp.where(diff <= 0, jnp.exp(lgb * jnp.maximum(-diff, 0.0)), 0.0))

    def fstep(c, carry):
        k = load(k_ref, c)
        v = load(v_ref, c)
        if not states_only:
            q = load(q_ref, c)
            scores = lax.dot_general(q, k, (((1,), (1,)), ((), ())), preferred_element_type=F32)
            o = _bdot((scores * dmask_scr[...]).astype(BF16), v)
            qf = q.astype(F32)
            q2 = jnp.concatenate([(qf * q_dec_f).astype(BF16), (qf * q_dec_b).astype(BF16)], axis=1)
            s2 = jnp.concatenate([s_scr[...].astype(BF16), sball_scr[c]], axis=0)
            o = o + _bdot(q2, s2)
            o = o * lax.rsqrt(jnp.mean(o * o, axis=-1, keepdims=True) + NORM_EPS)
            r0 = pl.multiple_of(c * cl, cl)
            o_ref[0, pl.ds(r0, cl), :] = (o * load(g_ref, c).astype(F32)).astype(BF16)
        s_scr[...] = s_scr[...] * chunk_dec_f + kv_outer(k, k_dec_f, v)
        return carry

    lax.fori_loop(0, nch, fstep, 0, unroll=RET_UNROLL)
    sf_ref[0, 0] = s_scr[...]


def _retention(qk, k_off, v, gate, ret_decay, s0f, s0b, b, states_only=False):
    h = v.shape[0]
    l = v.shape[1] // b
    cl = min(RET_CHUNK, l)
    nch = l // cl
    sspec = pl.BlockSpec((1, 1, QK_DIM, V_DIM), lambda i, j: (i, j, 0, 0))
    state_shape = jax.ShapeDtypeStruct((b, h, QK_DIM, V_DIM), F32)
    hspec = lambda w, off: pl.BlockSpec((1, 1, l, w), lambda i, j: (off + j, i, 0, 0))
    qk4 = qk.reshape(k_off + h, b, l, QK_DIM)
    v4 = v.reshape(h, b, l, V_DIM)
    smem = pl.BlockSpec(memory_space=pltpu.SMEM)
    kern = functools.partial(_ret_kernel, nch=nch, cl=cl, states_only=states_only)
    if states_only:
        return pl.pallas_call(
            kern, grid=(b, h),
            in_specs=[smem, hspec(QK_DIM, k_off), hspec(V_DIM, 0), sspec, sspec],
            out_specs=[sspec, sspec],
            out_shape=[state_shape, state_shape],
            scratch_shapes=[pltpu.VMEM((QK_DIM, V_DIM), F32)],
            compiler_params=_cparams(("arbitrary", "arbitrary")),
            name="retention_states",
        )(ret_decay, qk4, v4, s0f, s0b)
    return pl.pallas_call(
        kern, grid=(b, h),
        in_specs=[smem, hspec(QK_DIM, 0), hspec(QK_DIM, k_off), hspec(V_DIM, 0), hspec(V_DIM, 0),
                  sspec, sspec],
        out_specs=[pl.BlockSpec((1, l, V_DIM), lambda i, j: (i, 0, j)), sspec, sspec],
        out_shape=[jax.ShapeDtypeStruct((b, l, h * V_DIM), BF16), state_shape, state_shape],
        scratch_shapes=[pltpu.VMEM((QK_DIM, V_DIM), F32),
                        pltpu.VMEM((nch, QK_DIM, V_DIM), BF16),
                        pltpu.VMEM((cl, cl), F32)],
        compiler_params=_cparams(("arbitrary", "arbitrary")),
        name="retention",
    )(ret_decay, qk4, qk4, v4, gate.reshape(h, b, l, V_DIM), s0f, s0b)


def _merge_kernel(hy_ref, ret_ref, why_ref, wret_ref, ghy_ref, gret_ref, o_ref):
    a = _bdot(hy_ref[...], why_ref[...])
    b = _bdot(ret_ref[...], wret_ref[...])
    o_ref[...] = (ghy_ref[...].astype(F32) * a + gret_ref[...].astype(F32) * b).astype(BF16)


def _merge(hy, ret, pmg, w_hy, w_ret):
    mm, d = hy.shape
    tm = min(512, mm)
    tn = min(1024, d)
    nb = d // tn
    return pl.pallas_call(
        _merge_kernel,
        grid=(mm // tm, nb),
        in_specs=[
            pl.BlockSpec((tm, d), lambda i, j: (i, 0)),
            pl.BlockSpec((tm, 2 * d), lambda i, j: (i, 0)),
            pl.BlockSpec((d, tn), lambda i, j: (0, j)),
            pl.BlockSpec((2 * d, tn), lambda i, j: (0, j)),
            pl.BlockSpec((tm, tn), lambda i, j: (i, j)),
            pl.BlockSpec((tm, tn), lambda i, j: (i, nb + j)),
        ],
        out_specs=pl.BlockSpec((tm, tn), lambda i, j: (i, j)),
        out_shape=jax.ShapeDtypeStruct((mm, d), BF16),
        compiler_params=_cparams(("arbitrary", "arbitrary")),
        name="merge_proj",
    )(hy, ret, w_hy, w_ret, pmg, pmg)


def _wo_kernel(*refs, mode):
    if mode == "final":
        m_ref, w_ref, x_ref, g_ref, fg_ref, o_ref = refs
    else:
        m_ref, w_ref, x_ref, g_ref, ng_ref, nsc_ref, nsh_ref, o_ref, h_ref = refs
    xn = x_ref[0] + g_ref[0] * _bdot(m_ref[0], w_ref[...])
    y = xn * lax.rsqrt(jnp.mean(xn * xn, axis=-1, keepdims=True) + NORM_EPS)
    if mode == "final":
        o_ref[0] = y * fg_ref[...]
    else:
        o_ref[0] = xn
        h_ref[0] = ((y * ng_ref[...]) * (1.0 + nsc_ref[0]) + nsh_ref[0]).astype(BF16)


def _wo_residual(mrg, w_o, x, gate, final_g=None, next_norm=None):
    b, l, d = x.shape
    tl = min(512, l)
    tok = pl.BlockSpec((1, tl, d), lambda i, j: (i, j, 0))
    per_batch = pl.BlockSpec((1, 1, d), lambda i, j: (i, 0, 0))
    vec = pl.BlockSpec((1, d), lambda i, j: (0, 0))
    in_specs = [tok, pl.BlockSpec((d, d), lambda i, j: (0, 0)), tok, per_batch]
    args = [mrg.reshape(b, l, d), w_o, x, gate]
    if final_g is not None:
        mode = "final"
        in_specs.append(vec)
        args.append(final_g.reshape(1, d))
        out_specs, out_shape = tok, jax.ShapeDtypeStruct((b, l, d), F32)
    else:
        mode = "next"
        g, sc, sh = next_norm
        in_specs += [vec, per_batch, per_batch]
        args += [g.reshape(1, d), sc, sh]
        out_specs = [tok, tok]
        out_shape = [jax.ShapeDtypeStruct((b, l, d), F32), jax.ShapeDtypeStruct((b, l, d), BF16)]
    return pl.pallas_call(
        functools.partial(_wo_kernel, mode=mode),
        grid=(b, l // tl),
        in_specs=in_specs,
        out_specs=out_specs,
        out_shape=out_shape,
        compiler_params=_cparams(("arbitrary", "arbitrary")),
        name="wo_residual_" + mode,
    )(*args)


def _mixer(h, use_rope, s0f, s0b, w_in, layer, conv_w, conv_b, kf, consts, hy_bias, ret_decay,
           w_hy_b, w_ret_b):
    b, l, d = h.shape
    h2 = h.reshape(b * l, d)
    proj = functools.partial(_proj, h2, w_in, layer)
    qk = proj(0, 2 * d, l, "rope" if use_rope else "qkscale", hw=QK_DIM)
    v = proj(2 * d, 2 * d, l, "plain", hw=V_DIM)
    rg = proj(4 * d, 2 * d, l, "silu", hw=V_DIM)
    phy = proj(6 * d, 3 * d, l, "plain")
    phg = proj(9 * d, d, l, "silu")
    pmg = proj(10 * d, 2 * d, l, "sigmoid")
    u, m = _hyena_front(phy.reshape(b, l, 3 * d), phg.reshape(b, l, d), conv_w, conv_b)
    hy = _long_conv(u, m, hy_bias, kf, consts)
    ret, sf, sb = _retention(qk, d // QK_DIM, v, rg, ret_decay, s0f, s0b, b)
    mrg = _merge(hy.reshape(b * l, d), ret.reshape(b * l, 2 * d), pmg, w_hy_b, w_ret_b)
    return mrg, sf, sb


def kernel(x, c, ctx, c_ctx, ln_g, ada_w, ada_b, w_in, hy_conv_w, hy_conv_b, hy_filt_w1,
           hy_filt_b1, hy_filt_w2, hy_filt_b2, hy_filt_w3, hy_filt_b3, hy_filt_freq,
           hy_filt_wout, hy_bias, ret_decay, w_hy_out, w_ret_out, w_o, final_g):
    b, l, d = x.shape
    lc = ctx.shape[1]
    depth = ln_g.shape[0]
    h = d // QK_DIM
    assert b % 2 == 0 and d % QK_DIM == 0 and l % GRID_W == 0

    rows = -(-(b + 1) // 8) * 8
    cond = jnp.zeros((rows, d), F32).at[:b].set(c).at[b].set(c_ctx)
    mod = _ada_modulation(cond, ada_w, ada_b)

    consts_l = _dft_consts(l, FFT_N2 if (2 * l) % FFT_N2 == 0 and l >= 1024 else 1)
    consts_c = _dft_consts(lc, 1)
    zero_state = jnp.zeros((b, h, QK_DIM, V_DIM), F32)

    lat = lambda a: a[:b, None, :]
    cx = lambda a: jnp.broadcast_to(a[b][None, None, :], (b, 1, d))
    shift = lambda i: mod[i, :, :d]
    scale = lambda i: mod[i, :, d:2 * d]
    h_ctx = _prenorm(ctx, ln_g[0], cx(scale(0)), cx(shift(0)))
    h_lat = _prenorm(x, ln_g[0], lat(scale(0)), lat(shift(0)))

    for i in range(depth):
        gt = mod[i, :, 2 * d:]
        last = i == depth - 1
        w_hy_b = w_hy_out[i].astype(BF16)
        w_ret_b = w_ret_out[i].astype(BF16)
        w_o_b = w_o[i].astype(BF16)
        filt = (hy_filt_w1[i], hy_filt_b1[i], hy_filt_w2[i], hy_filt_b2[i], hy_filt_w3[i],
                hy_filt_b3[i], hy_filt_freq[i], hy_filt_wout[i])
        params = lambda kf, consts: (w_in, i, hy_conv_w[i], hy_conv_b[i], kf, consts, hy_bias[i],
                                     ret_decay[i], w_hy_b, w_ret_b)

        if not last:
            kf_c = _filter_spectrum(_hyena_filter_taps(lc, d, *filt), consts_c)
            mrg_c, s_ctx_f, s_ctx_b = _mixer(h_ctx, False, zero_state, zero_state,
                                             *params(kf_c, consts_c))
            ctx, h_ctx = _wo_residual(mrg_c, w_o_b, ctx, cx(gt),
                                      next_norm=(ln_g[i + 1], cx(scale(i + 1)), cx(shift(i + 1))))
        else:
            hc2 = h_ctx.reshape(b * lc, d)
            k_c = _proj(hc2, w_in, i, d, d, lc, "qkscale", hw=QK_DIM)
            v_c = _proj(hc2, w_in, i, 2 * d, 2 * d, lc, "plain", hw=V_DIM)
            s_ctx_f, s_ctx_b = _retention(k_c, 0, v_c, None, ret_decay[i], zero_state, zero_state,
                                          b, states_only=True)

        kf_l = _filter_spectrum(_hyena_filter_taps(l, d, *filt), consts_l)
        mrg, _, _ = _mixer(h_lat, True, s_ctx_f, s_ctx_b, *params(kf_l, consts_l))
        if last:
            x = _wo_residual(mrg, w_o_b, x, lat(gt), final_g=final_g)
        else:
            x, h_lat = _wo_residual(mrg, w_o_b, x, lat(gt),
                                    next_norm=(ln_g[i + 1], lat(scale(i + 1)), lat(shift(i + 1))))

    return x
```

```python
import functools
import math

import numpy as np
import jax
import jax.numpy as jnp
from jax import lax
from jax.experimental import pallas as pl
from jax.experimental.pallas import tpu as pltpu

F32 = jnp.float32
BF16 = jnp.bfloat16

NORM_EPS = 1e-6
GRID_W = 64
ROPE_BASE = 10000.0
QK_DIM = 256
V_DIM = 512
RET_CHUNK = 256
RET_UNROLL = 2
HY_EMB = 33
HY_BANDS = (HY_EMB - 1) // 2
HY_EMB_PAD = 64
HY_FAST_DECAY_PCT = 0.3
HY_SLOW_DECAY_PCT = 1.5
HY_DECAY_TARGET = 1e-2
FFT_N2 = 64
PROJ_NSUB = 4
LANES = 128
FFT_PITCH = 72
FFT_UNROLL = 16
VMEM_LIMIT = 56 * 1024 * 1024


def _cparams(sem):
    return pltpu.CompilerParams(dimension_semantics=sem, vmem_limit_bytes=VMEM_LIMIT)


def _silu(x):
    return x / (1.0 + jnp.exp(-x))


def _sigmoid(x):
    return 1.0 / (1.0 + jnp.exp(-x))


def _bdot(a, b):
    return jnp.dot(a, b, preferred_element_type=F32)


def _ada_kernel(c_ref, w_ref, b_ref, o_ref):
    s = _silu(c_ref[...]).astype(BF16)
    o_ref[0] = _bdot(s, w_ref[0].astype(BF16)) + b_ref[0]


def _ada_modulation(cond, ada_w, ada_b):
    depth, d, w3 = ada_w.shape
    r = cond.shape[0]
    tn = min(512, w3)
    return pl.pallas_call(
        _ada_kernel,
        grid=(depth, w3 // tn),
        in_specs=[
            pl.BlockSpec((r, d), lambda l, j: (0, 0)),
            pl.BlockSpec((1, d, tn), lambda l, j: (l, 0, j)),
            pl.BlockSpec((1, 1, tn), lambda l, j: (l, 0, j)),
        ],
        out_specs=pl.BlockSpec((1, r, tn), lambda l, j: (l, 0, j)),
        out_shape=jax.ShapeDtypeStruct((depth, r, w3), F32),
        compiler_params=_cparams(("arbitrary", "arbitrary")),
        name="ada_mod",
    )(cond, ada_w, ada_b.reshape(depth, 1, w3))


def _prenorm_kernel(x_ref, g_ref, sc_ref, sh_ref, o_ref):
    x = x_ref[0]
    y = x * lax.rsqrt(jnp.mean(x * x, axis=-1, keepdims=True) + NORM_EPS)
    o_ref[0] = ((y * g_ref[...]) * (1.0 + sc_ref[0]) + sh_ref[0]).astype(BF16)


def _prenorm(x, g, sc, sh):
    b, l, d = x.shape
    tl = min(512, l)
    return pl.pallas_call(
        _prenorm_kernel,
        grid=(b, l // tl),
        in_specs=[
            pl.BlockSpec((1, tl, d), lambda i, j: (i, j, 0)),
            pl.BlockSpec((1, d), lambda i, j: (0, 0)),
            pl.BlockSpec((1, 1, d), lambda i, j: (i, 0, 0)),
            pl.BlockSpec((1, 1, d), lambda i, j: (i, 0, 0)),
        ],
        out_specs=pl.BlockSpec((1, tl, d), lambda i, j: (i, j, 0)),
        out_shape=jax.ShapeDtypeStruct((b, l, d), BF16),
        compiler_params=_cparams(("arbitrary", "arbitrary")),
        name="prenorm",
    )(x, g.reshape(1, d), sc, sh)


def _proj_kernel(*refs, epi, nsub, tm, tn, r, hw, j0):
    if epi == "rope":
        h_ref, w_ref, cos_ref, sin_ref, o_ref, wb_scr = refs
    else:
        h_ref, w_ref, o_ref, wb_scr = refs

    @pl.when(pl.program_id(1) == 0)
    def _():
        wb_scr[...] = w_ref[...].astype(BF16)

    qk_scale = jnp.where(j0 + pl.program_id(0) < r, 1.0, QK_DIM ** -0.5).astype(F32)
    ts = tm // nsub
    sw = 128 if epi == "rope" else (hw or tn)
    for s in range(nsub):
        rows = slice(s * ts, (s + 1) * ts)
        acc = _bdot(h_ref[rows, :], wb_scr[...])
        for g in range(tn // sw):
            val = acc[:, g * sw:(g + 1) * sw]
            if epi == "rope":
                t = (g % 2) * 128
                val = (val * cos_ref[rows, t:t + 128]
                       + pltpu.roll(val, 64, 1) * sin_ref[rows, t:t + 128]) * qk_scale
            elif epi == "qkscale":
                val = val * qk_scale
            elif epi == "silu":
                val = _silu(val)
            elif epi == "sigmoid":
                val = _sigmoid(val)
            val = val.astype(BF16)
            if hw:
                c0 = g * sw
                o_ref[c0 // hw, rows, c0 % hw:c0 % hw + sw] = val
            else:
                o_ref[rows, g * sw:(g + 1) * sw] = val


def _rope_tables(l):
    quarter = QK_DIM // 4
    inv = 1.0 / (ROPE_BASE ** (jnp.arange(quarter, dtype=F32) / quarter))
    t = jnp.arange(l)
    row = (t // GRID_W).astype(F32)
    col = (t % GRID_W).astype(F32)
    ar = row[:, None] * inv[None, :]
    ac = col[:, None] * inv[None, :]
    cos_t = jnp.concatenate([jnp.cos(ar), jnp.cos(ar), jnp.cos(ac), jnp.cos(ac)], axis=-1)
    sin_t = jnp.concatenate([-jnp.sin(ar), jnp.sin(ar), -jnp.sin(ac), jnp.sin(ac)], axis=-1)
    return cos_t, sin_t


def _proj(h, w, layer, col0, ncols, l, epi, hw=None):
    m, d = h.shape
    tm = min(1024, m)
    tn = min(1024, d)
    j0 = col0 // tn
    assert epi != "rope" or l % tm == 0
    in_specs = [
        pl.BlockSpec((tm, d), lambda j, i: (i, 0)),
        pl.BlockSpec((None, d, tn), lambda j, i: (layer, 0, j0 + j)),
    ]
    args = [h, w]
    if epi == "rope":
        lb = l // tm
        in_specs += [pl.BlockSpec((tm, QK_DIM), lambda j, i: (i % lb, 0))] * 2
        args += list(_rope_tables(l))
    if hw:
        out_spec = pl.BlockSpec((tn // hw, tm, hw), lambda j, i: (j, i, 0))
        out_shape = jax.ShapeDtypeStruct((ncols // hw, m, hw), BF16)
    else:
        out_spec = pl.BlockSpec((tm, tn), lambda j, i: (i, j))
        out_shape = jax.ShapeDtypeStruct((m, ncols), BF16)
    return pl.pallas_call(
        functools.partial(_proj_kernel, epi=epi, nsub=PROJ_NSUB, tm=tm, tn=tn, r=d // tn, hw=hw,
                          j0=j0),
        grid=(ncols // tn, m // tm),
        in_specs=in_specs,
        out_specs=out_spec,
        out_shape=out_shape,
        scratch_shapes=[pltpu.VMEM((d, tn), BF16)],
        compiler_params=_cparams(("arbitrary", "arbitrary")),
        name="in_proj_" + epi,
    )(*args)


def _hyfront_kernel(v_ref, x0_ref, x1_ref, g_ref, wv_ref, w0_ref, w1_ref,
                    bv_ref, b0_ref, b1_ref, u_ref, m_ref, *, l):
    rows = lax.broadcasted_iota(jnp.int32, (l, 1), 0)

    def conv3(x_ref, w_ref, b_ref):
        x = x_ref[0].astype(F32)
        prev = jnp.where(rows == 0, 0.0, pltpu.roll(x, 1, 0))
        nxt = jnp.where(rows == l - 1, 0.0, pltpu.roll(x, l - 1, 0))
        return prev * w_ref[0:1, :] + x * w_ref[1:2, :] + nxt * w_ref[2:3, :] + b_ref[...]

    hv = conv3(v_ref, wv_ref, bv_ref)
    hx1 = conv3(x1_ref, w1_ref, b1_ref)
    u_ref[0] = (hv * hx1).astype(BF16)
    hx0 = conv3(x0_ref, w0_ref, b0_ref)
    m_ref[0] = (hx0 * g_ref[0].astype(F32)).astype(BF16)


def _hyena_front(phy, phg, conv_w, conv_b):
    b, l, d = phg.shape
    cs = 128
    nb = d // cs
    pspec = lambda off: pl.BlockSpec((1, l, cs), lambda i, j: (i, 0, off * nb + j))
    wspec = lambda off: pl.BlockSpec((3, cs), lambda i, j: (0, off * nb + j))
    bspec = lambda off: pl.BlockSpec((1, cs), lambda i, j: (0, off * nb + j))
    ospec = pl.BlockSpec((1, l, cs), lambda i, j: (i, 0, j))
    return pl.pallas_call(
        functools.partial(_hyfront_kernel, l=l),
        grid=(b, nb),
        in_specs=[pspec(0), pspec(1), pspec(2), pspec(0),
                  wspec(0), wspec(1), wspec(2), bspec(0), bspec(1), bspec(2)],
        out_specs=[ospec, ospec],
        out_shape=[jax.ShapeDtypeStruct((b, l, d), BF16)] * 2,
        compiler_params=_cparams(("arbitrary", "arbitrary")),
        name="hyena_front",
    )(phy, phy, phy, phg, conv_w, conv_w, conv_w,
      conv_b.reshape(1, -1), conv_b.reshape(1, -1), conv_b.reshape(1, -1))


def _filter_kernel(z_ref, t_ref, w1_ref, b1_ref, w2_ref, b2_ref, w3_ref, b3_ref, fr_ref,
                   wo_ref, dl_ref, o_ref, *, tr, l):
    dot = functools.partial(jnp.dot, precision=lax.Precision.HIGHEST, preferred_element_type=F32)
    f = fr_ref[...]
    h = jnp.sin(f * (dot(w1_ref[...], z_ref[...]) + b1_ref[...]))
    h = jnp.sin(f * (dot(w2_ref[...], h) + b2_ref[...]))
    h = jnp.sin(f * (dot(w3_ref[...], h) + b3_ref[...]))
    y = lax.dot_general(h.astype(BF16), wo_ref[...].astype(BF16), (((0,), (0,)), ((), ())),
                        preferred_element_type=F32)
    win = jnp.exp(-t_ref[...] * dl_ref[...])
    rows = pl.program_id(1) * tr + lax.broadcasted_iota(jnp.int32, (tr, 1), 0)
    o_ref[...] = jnp.where(rows == l, 0.0, y * win).astype(BF16)


def _hyena_filter_taps(l, c, w1, b1, w2, b2, w3, b3, freq, w_out):
    t = jnp.linspace(0.0, 1.0, l, dtype=F32)[:, None]
    ang = 2.0 * math.pi * jnp.arange(l, dtype=F32)[:, None] / l
    f = jnp.linspace(1e-4, HY_BANDS - 1, HY_BANDS, dtype=F32)[None, :]
    z = jnp.concatenate([t, jnp.cos(f * ang), -jnp.sin(f * ang)], axis=-1)
    z = jnp.pad(z, ((0, 0), (0, HY_EMB_PAD - HY_EMB)))
    back = lambda a: jnp.concatenate([a[l - 1:l], a[:0:-1]], axis=0)
    z2 = jnp.concatenate([z, back(z)], axis=0)
    t2 = jnp.concatenate([t, back(t)], axis=0)
    max_decay = math.log(HY_DECAY_TARGET) / HY_FAST_DECAY_PCT
    min_decay = math.log(HY_DECAY_TARGET) / HY_SLOW_DECAY_PCT
    deltas = jnp.abs(jnp.linspace(min_decay, max_decay, c, dtype=F32))[None, :]
    w1p = jnp.pad(w1, ((0, 0), (0, HY_EMB_PAD - HY_EMB), (0, 0)))
    hid = w2.shape[1]
    tr = min(512, l)
    lb = l // tr
    depth = w1.shape[0]
    per_layer = lambda r, cc: pl.BlockSpec((None, r, cc), lambda g, i: (g, 0, 0))
    col = lambda a: a[:, :, None]
    tp = lambda a: jnp.swapaxes(a, 1, 2)
    return pl.pallas_call(
        functools.partial(_filter_kernel, tr=tr, l=l),
        grid=(depth, 2 * lb),
        in_specs=[
            pl.BlockSpec((HY_EMB_PAD, tr), lambda g, i: (0, i)),
            pl.BlockSpec((tr, 1), lambda g, i: (i, 0)),
            per_layer(hid, HY_EMB_PAD), per_layer(hid, 1),
            per_layer(hid, hid), per_layer(hid, 1),
            per_layer(hid, hid), per_layer(hid, 1),
            per_layer(hid, 1),
            pl.BlockSpec((None, hid, c), lambda g, i: (g, 0, i // lb)),
            pl.BlockSpec((1, c), lambda g, i: (0, 0)),
        ],
        out_specs=pl.BlockSpec((None, tr, c), lambda g, i: (g, i, 0)),
        out_shape=jax.ShapeDtypeStruct((depth, 2 * l, c), BF16),
        compiler_params=_cparams(("arbitrary", "arbitrary")),
        name="hyena_filter",
    )(z2.T, t2, tp(w1p), col(b1), tp(w2), col(b2), tp(w3), col(b3), col(freq), w_out, deltas)


def _dft_consts(l, n2):
    n = 2 * l
    n1 = n // n2
    nin = n1 // 2

    def cs(k, m, period):
        ph = 2.0 * np.pi * ((np.outer(k, m)) % period) / period
        return np.cos(ph), np.sin(ph)

    k1 = np.arange(n1)
    c, s = cs(k1, np.arange(nin), n1)
    f1 = np.block([[c, s], [-s, c]])
    cf, sf = cs(k1, np.arange(n1), n1)
    f1_real = np.concatenate([cf, -sf], axis=0)
    ci, si = cs(np.arange(nin), k1, n1)
    g1 = np.block([[ci, -si], [si, ci]]) / n
    out = dict(n1=n1, n2=n2, nin=nin,
               f1=jnp.asarray(f1, F32).astype(BF16),
               f1_real=jnp.asarray(f1_real, F32).astype(BF16),
               g1=jnp.asarray(g1, F32).astype(BF16))
    if n2 > 1:
        f1_il = np.empty_like(f1)
        f1_il[0::2], f1_il[1::2] = f1[:n1], f1[n1:]
        out["f1_il"] = jnp.asarray(f1_il, F32).astype(BF16)
        out["g1_il"] = jnp.asarray(f1_il.T / n, F32).astype(BF16)
        f1r_il = np.empty_like(f1_real)
        f1r_il[0::2], f1r_il[1::2] = f1_real[:n1], f1_real[n1:]
        out["f1_real_il"] = jnp.asarray(f1r_il, F32).astype(BF16)
        m2 = np.arange(n2)
        ph = (m2[None, None, :] * k1[:, None, None] + n1 * m2[None, None, :] * m2[None, :, None]) % n
        ph = 2.0 * np.pi * ph / n
        tr, ti = np.cos(ph), -np.sin(ph)
        t_fwd = np.concatenate([np.concatenate([tr, -ti], axis=2),
                                np.concatenate([ti, tr], axis=2)], axis=1)
        t_il = np.empty_like(t_fwd)
        t_il[:, :, 0::2], t_il[:, :, 1::2] = t_fwd[:, :, :n2], t_fwd[:, :, n2:]
        out["t_il"] = jnp.asarray(t_il, F32).astype(BF16)
    return out


def _cmm_kernel(*refs, mode, chunk, nchunk):
    a_ref, x_ref, o_ref = refs[0], refs[1], refs[-1]
    a = a_ref[...]
    hm = a.shape[0] // 2
    for cc in range(nchunk):
        sl = slice(cc * chunk, (cc + 1) * chunk)
        acc = _bdot(a, x_ref[0, :, sl])
        if mode == "kfmul":
            kf_ref = refs[2]
            xr, xi = acc[:hm], acc[hm:]
            kr, ki = kf_ref[0, :, sl], kf_ref[1, :, sl]
            o_ref[0, :hm, sl] = (xr * kr - xi * ki).astype(o_ref.dtype)
            o_ref[0, hm:, sl] = (xr * ki + xi * kr).astype(o_ref.dtype)
        elif mode == "epi":
            u_ref, m_ref, b_ref = refs[2], refs[3], refs[4]
            u = u_ref[0, :, sl].astype(F32)
            o = (acc + b_ref[:, sl] * u) * m_ref[0, :, sl].astype(F32)
            o_ref[0, :, sl] = o.astype(o_ref.dtype)
        else:
            o_ref[0, :, sl] = acc.astype(o_ref.dtype)


def _cmm(a, x, mode="plain", extra=(), out_dtype=BF16):
    mr, k = a.shape
    g, _, cols = x.shape
    tc = min(8192, cols)
    chunk = min(1024, tc)
    in_specs = [pl.BlockSpec((mr, k), lambda i, j: (0, 0)),
                pl.BlockSpec((1, k, tc), lambda i, j: (i, 0, j))]
    if mode == "kfmul":
        in_specs += [pl.BlockSpec((2, mr // 2, tc), lambda i, j: (0, 0, j))]
    elif mode == "epi":
        in_specs += [pl.BlockSpec((1, mr, tc), lambda i, j: (i, 0, j)),
                     pl.BlockSpec((1, mr, tc), lambda i, j: (i, 0, j)),
                     pl.BlockSpec((1, tc), lambda i, j: (0, j))]
    return pl.pallas_call(
        functools.partial(_cmm_kernel, mode=mode, chunk=chunk, nchunk=tc // chunk),
        grid=(g, cols // tc),
        in_specs=in_specs,
        out_specs=pl.BlockSpec((1, mr, tc), lambda i, j: (i, 0, j)),
        out_shape=jax.ShapeDtypeStruct((g, mr, cols), out_dtype),
        compiler_params=_cparams(("arbitrary", "arbitrary")),
        name="dft_" + mode,
    )(a, x, *extra)


def _filter_fft_kernel(x_ref, f1_ref, t_ref, o_ref, s1, ab, *, n1, n2, nf):
    lanes = lambda f: jnp.concatenate([f(s) for s in range(nf)], axis=1)

    for s in range(nf):
        def fill(i, carry, s=s):
            src = pl.ds(pl.multiple_of(i * n2, n2), n2)
            s1[s, pl.ds(pl.multiple_of(i * FFT_PITCH, 8), n2), :] = x_ref[s, src, :].astype(F32)
            return carry
        lax.fori_loop(0, n1, fill, 0, unroll=8)

    def stage1(j, carry):
        rows = pl.ds(j, n1, stride=FFT_PITCH)
        z = lanes(lambda s: s1[s, rows, :]).astype(BF16)
        a = pltpu.bitcast(_bdot(f1_ref[...], z).astype(BF16), jnp.uint32)
        for s in range(nf):
            ab[s, rows, :] = a[:, s * LANES:(s + 1) * LANES]
        return carry

    lax.fori_loop(0, n2, stage1, 0, unroll=FFT_UNROLL)

    def mid(k, carry):
        rows = pl.ds(pl.multiple_of(k * FFT_PITCH, 8), n2)
        a = pltpu.bitcast(lanes(lambda s: ab[s, rows, :]), BF16)
        x = _bdot(t_ref[k], a).astype(BF16)
        for s in range(nf):
            o_ref[s, 0, k] = x[:n2, s * LANES:(s + 1) * LANES]
            o_ref[s, 1, k] = x[n2:, s * LANES:(s + 1) * LANES]
        return carry

    lax.fori_loop(0, n1, mid, 0, unroll=FFT_UNROLL)


def _filter_spectrum(taps, consts):
    nf, n, c = taps.shape
    n1, n2 = consts["n1"], consts["n2"]
    if n2 == 1:
        return _cmm(consts["f1_real"], taps, out_dtype=F32).reshape(nf, 2, n1, c)
    const = lambda shape: pl.BlockSpec(shape, lambda j: (0,) * len(shape))
    return pl.pallas_call(
        functools.partial(_filter_fft_kernel, n1=n1, n2=n2, nf=nf),
        grid=(c // LANES,),
        in_specs=[pl.BlockSpec((nf, n, LANES), lambda j: (0, 0, j)),
                  const((2 * n1, n1)), const((n1, 2 * n2, 2 * n2))],
        out_specs=pl.BlockSpec((nf, 2, n1, n2, LANES), lambda j: (0, 0, 0, 0, j)),
        out_shape=jax.ShapeDtypeStruct((nf, 2, n1, n2, c), BF16),
        scratch_shapes=[pltpu.VMEM((nf, n1 * FFT_PITCH, LANES), F32),
                        pltpu.VMEM((nf, n1 * FFT_PITCH, LANES), jnp.uint32)],
        compiler_params=_cparams(("arbitrary",)),
        name="filter_fft",
    )(taps, consts["f1_real_il"], consts["t_il"])


def _conv_kernel(u_ref, m_ref, hb_ref, kf_ref, f1_ref, t_ref, g1_ref, o_ref, s1, ab, *, n1, n2, nb):
    nin = n1 // 2
    npair = nb // 2
    lanes = lambda f: jnp.concatenate([f(s) for s in range(npair)], axis=1)

    for bi in range(nb):
        def fill(i, carry, bi=bi):
            src = pl.ds(pl.multiple_of(i * n2, n2), n2)
            dst = pl.ds(pl.multiple_of(i * FFT_PITCH, 8), n2)
            s1[bi % 2, bi // 2, dst, :] = u_ref[bi, src, :].astype(F32)
            return carry
        lax.fori_loop(0, nin, fill, 0, unroll=8)

    def stage1(j, carry):
        rows = pl.ds(j, nin, stride=FFT_PITCH)
        z = jnp.concatenate([lanes(lambda s: s1[0, s, rows, :]),
                             lanes(lambda s: s1[1, s, rows, :])], axis=0).astype(BF16)
        a = pltpu.bitcast(_bdot(f1_ref[...], z).astype(BF16), jnp.uint32)
        for s in range(npair):
            ab[s, pl.ds(j, n1, stride=FFT_PITCH), :] = a[:, s * LANES:(s + 1) * LANES]
        return carry

    lax.fori_loop(0, n2, stage1, 0, unroll=FFT_UNROLL)

    def mid(k, carry):
        rows = pl.ds(pl.multiple_of(k * FFT_PITCH, 8), n2)
        a = pltpu.bitcast(lanes(lambda s: ab[s, rows, :]), BF16)
        t = t_ref[k]
        x = _bdot(t, a)
        xr, xi = x[:n2], x[n2:]
        kr = lanes(lambda s: kf_ref[0, k].astype(F32))
        ki = lanes(lambda s: kf_ref[1, k].astype(F32))
        y = jnp.concatenate([xr * kr - xi * ki, xr * ki + xi * kr], axis=0).astype(BF16)
        bm = lax.dot_general(t, y, (((0,), (0,)), ((), ())), preferred_element_type=F32)
        w = pltpu.bitcast(bm.astype(BF16), jnp.uint32)
        for s in range(npair):
            ab[s, rows, :] = w[:, s * LANES:(s + 1) * LANES]
        return carry

    lax.fori_loop(0, n1, mid, 0, unroll=FFT_UNROLL)

    def stage4(j, carry):
        b = pltpu.bitcast(lanes(lambda s: ab[s, pl.ds(j, n1, stride=FFT_PITCH), :]), BF16)
        y = _bdot(g1_ref[...], b)
        rows = pl.ds(j, nin, stride=FFT_PITCH)
        for s in range(npair):
            s1[0, s, rows, :] = y[:nin, s * LANES:(s + 1) * LANES]
            s1[1, s, rows, :] = y[nin:, s * LANES:(s + 1) * LANES]
        return carry

    lax.fori_loop(0, n2, stage4, 0, unroll=FFT_UNROLL)

    for bi in range(nb):
        def emit(i, carry, bi=bi):
            dst = pl.ds(pl.multiple_of(i * n2, n2), n2)
            src = pl.ds(pl.multiple_of(i * FFT_PITCH, 8), n2)
            y = s1[bi % 2, bi // 2, src, :] + hb_ref[...] * u_ref[bi, dst, :].astype(F32)
            o_ref[bi, dst, :] = (y * m_ref[bi, dst, :].astype(F32)).astype(BF16)
            return carry
        lax.fori_loop(0, nin, emit, 0, unroll=8)


def _long_conv(u, m, bias, kf, layer, consts):
    b, l, c = u.shape
    n1, n2, nin = consts["n1"], consts["n2"], consts["nin"]
    p = b // 2
    if n2 == 1:
        x = u.reshape(p, 2 * nin, c)
        y = _cmm(consts["f1"], x, mode="kfmul", extra=(kf[layer],))
        extra = (x, m.reshape(p, 2 * nin, c), bias.reshape(1, c))
        return _cmm(consts["g1"], y, mode="epi", extra=extra).reshape(b, l, c)
    const = lambda shape: pl.BlockSpec(shape, lambda j: (0,) * len(shape))
    return pl.pallas_call(
        functools.partial(_conv_kernel, n1=n1, n2=n2, nb=b),
        grid=(c // LANES,),
        in_specs=[pl.BlockSpec((b, l, LANES), lambda j: (0, 0, j)),
                  pl.BlockSpec((b, l, LANES), lambda j: (0, 0, j)),
                  pl.BlockSpec((1, LANES), lambda j: (0, j)),
                  pl.BlockSpec((None, 2, n1, n2, LANES), lambda j: (layer, 0, 0, 0, j)),
                  const((2 * n1, n1)), const((n1, 2 * n2, 2 * n2)), const((n1, 2 * n1))],
        out_specs=pl.BlockSpec((b, l, LANES), lambda j: (0, 0, j)),
        out_shape=jax.ShapeDtypeStruct((b, l, c), BF16),
        scratch_shapes=[pltpu.VMEM((2, p, nin * FFT_PITCH, LANES), F32),
                        pltpu.VMEM((p, n1 * FFT_PITCH, LANES), jnp.uint32)],
        compiler_params=_cparams(("arbitrary",)),
        name="long_conv",
    )(u, m, bias.reshape(1, c), kf, consts["f1_il"], consts["t_il"], consts["g1_il"])


def _ret_kernel(*refs, nch, cl, states_only):
    if states_only:
        rd_ref, k_ref, v_ref, s0f_ref, s0b_ref, sf_ref, sb_ref, s_scr = refs
    else:
        (rd_ref, q_ref, k_ref, v_ref, g_ref, s0f_ref, s0b_ref,
         o_ref, sf_ref, sb_ref, s_scr, sball_scr, dmask_scr) = refs
    hd = pl.program_id(1)
    one = jnp.ones((1, 1), F32)
    lgf = -jnp.exp(one * rd_ref[0, hd])
    lgb = -jnp.exp(one * rd_ref[1, hd])
    ri = lax.broadcasted_iota(jnp.int32, (cl, 1), 0).astype(F32)
    ci = lax.broadcasted_iota(jnp.int32, (1, cl), 1).astype(F32)
    per_token = lambda dec: jnp.broadcast_to(dec, (cl, QK_DIM)).astype(BF16)
    k_dec_f = per_token(jnp.exp(lgf * (cl - 1.0 - ri)))
    k_dec_b = per_token(jnp.exp(lgb * ri))
    chunk_dec_f = jnp.exp(lgf * cl)
    chunk_dec_b = jnp.exp(lgb * cl)

    def kv_outer(k, dec, v):
        return lax.dot_general(k * dec, v, (((0,), (0,)), ((), ())), preferred_element_type=F32)

    def load(ref, c):
        return ref[0, 0, pl.ds(pl.multiple_of(c * cl, cl), cl), :]

    s_scr[...] = s0b_ref[0, 0]

    def bstep(t, carry):
        c = nch - 1 - t
        if not states_only:
            sball_scr[c] = s_scr[...].astype(BF16)
        s_scr[...] = (s_scr[...] * chunk_dec_b
                      + kv_outer(load(k_ref, c), k_dec_b, load(v_ref, c)))
        return carry

    lax.fori_loop(0, nch, bstep, 0, unroll=2 * RET_UNROLL)
    sb_ref[0, 0] = s_scr[...]

    s_scr[...] = s0f_ref[0, 0]
    if not states_only:
        q_dec_f = per_token(jnp.exp(lgf * (ri + 1.0)))
        q_dec_b = per_token(jnp.exp(lgb * (cl - ri)))
        diff = ri - ci
        dmask_scr[...] = (jnp.where(diff >= 0, jnp.exp(lgf * jnp.maximum(diff, 0.0)), 0.0)
                          + jnp.where(diff <= 0, jnp.exp(lgb * jnp.maximum(-diff, 0.0)), 0.0))

    def fstep(c, carry):
        k = load(k_ref, c)
        v = load(v_ref, c)
        if not states_only:
            q = load(q_ref, c)
            scores = lax.dot_general(q, k, (((1,), (1,)), ((), ())), preferred_element_type=F32)
            o = _bdot((scores * dmask_scr[...]).astype(BF16), v)
            q2 = jnp.concatenate([q * q_dec_f, q * q_dec_b], axis=1)
            s2 = jnp.concatenate([s_scr[...].astype(BF16), sball_scr[c]], axis=0)
            o = o + _bdot(q2, s2)
            o = o * lax.rsqrt(jnp.mean(o * o, axis=-1, keepdims=True) + NORM_EPS)
            r0 = pl.multiple_of(c * cl, cl)
            o_ref[0, pl.ds(r0, cl), :] = o.astype(BF16) * load(g_ref, c)
        s_scr[...] = s_scr[...] * chunk_dec_f + kv_outer(k, k_dec_f, v)
        return carry

    lax.fori_loop(0, nch, fstep, 0, unroll=RET_UNROLL)
    sf_ref[0, 0] = s_scr[...]


def _retention(qk, k_off, v, gate, ret_decay, s0f, s0b, b, states_only=False):
    h = v.shape[0]
    l = v.shape[1] // b
    cl = min(RET_CHUNK, l)
    nch = l // cl
    sspec = pl.BlockSpec((1, 1, QK_DIM, V_DIM), lambda i, j: (i, j, 0, 0))
    state_shape = jax.ShapeDtypeStruct((b, h, QK_DIM, V_DIM), F32)
    hspec = lambda w, off: pl.BlockSpec((1, 1, l, w), lambda i, j: (off + j, i, 0, 0))
    qk4 = qk.reshape(k_off + h, b, l, QK_DIM)
    v4 = v.reshape(h, b, l, V_DIM)
    smem = pl.BlockSpec(memory_space=pltpu.SMEM)
    kern = functools.partial(_ret_kernel, nch=nch, cl=cl, states_only=states_only)
    if states_only:
        return pl.pallas_call(
            kern, grid=(b, h),
            in_specs=[smem, hspec(QK_DIM, k_off), hspec(V_DIM, 0), sspec, sspec],
            out_specs=[sspec, sspec],
            out_shape=[state_shape, state_shape],
            scratch_shapes=[pltpu.VMEM((QK_DIM, V_DIM), F32)],
            compiler_params=_cparams(("arbitrary", "arbitrary")),
            name="retention_states",
        )(ret_decay, qk4, v4, s0f, s0b)
    return pl.pallas_call(
        kern, grid=(b, h),
        in_specs=[smem, hspec(QK_DIM, 0), hspec(QK_DIM, k_off), hspec(V_DIM, 0), hspec(V_DIM, 0),
                  sspec, sspec],
        out_specs=[pl.BlockSpec((1, l, V_DIM), lambda i, j: (i, 0, j)), sspec, sspec],
        out_shape=[jax.ShapeDtypeStruct((b, l, h * V_DIM), BF16), state_shape, state_shape],
        scratch_shapes=[pltpu.VMEM((QK_DIM, V_DIM), F32),
                        pltpu.VMEM((nch, QK_DIM, V_DIM), BF16),
                        pltpu.VMEM((cl, cl), F32)],
        compiler_params=_cparams(("arbitrary", "arbitrary")),
        name="retention",
    )(ret_decay, qk4, qk4, v4, gate.reshape(h, b, l, V_DIM), s0f, s0b)


def _merge_kernel(hy_ref, ret_ref, why_ref, wret_ref, ghy_ref, gret_ref, o_ref):
    a = _bdot(hy_ref[...], why_ref[...])
    b = _bdot(ret_ref[...], wret_ref[...])
    o_ref[...] = (ghy_ref[...].astype(F32) * a + gret_ref[...].astype(F32) * b).astype(BF16)


def _merge(hy, ret, pmg, w_hy, w_ret):
    mm, d = hy.shape
    tm = min(512, mm)
    tn = min(1024, d)
    nb = d // tn
    return pl.pallas_call(
        _merge_kernel,
        grid=(mm // tm, nb),
        in_specs=[
            pl.BlockSpec((tm, d), lambda i, j: (i, 0)),
            pl.BlockSpec((tm, 2 * d), lambda i, j: (i, 0)),
            pl.BlockSpec((d, tn), lambda i, j: (0, j)),
            pl.BlockSpec((2 * d, tn), lambda i, j: (0, j)),
            pl.BlockSpec((tm, tn), lambda i, j: (i, j)),
            pl.BlockSpec((tm, tn), lambda i, j: (i, nb + j)),
        ],
        out_specs=pl.BlockSpec((tm, tn), lambda i, j: (i, j)),
        out_shape=jax.ShapeDtypeStruct((mm, d), BF16),
        compiler_params=_cparams(("arbitrary", "arbitrary")),
        name="merge_proj",
    )(hy, ret, w_hy, w_ret, pmg, pmg)


def _wo_kernel(*refs, mode):
    if mode == "final":
        m_ref, w_ref, x_ref, g_ref, fg_ref, o_ref = refs
    else:
        m_ref, w_ref, x_ref, g_ref, ng_ref, nsc_ref, nsh_ref, o_ref, h_ref = refs
    xn = x_ref[0] + g_ref[0] * _bdot(m_ref[0], w_ref[...])
    y = xn * lax.rsqrt(jnp.mean(xn * xn, axis=-1, keepdims=True) + NORM_EPS)
    if mode == "final":
        o_ref[0] = y * fg_ref[...]
    else:
        o_ref[0] = xn
        h_ref[0] = ((y * ng_ref[...]) * (1.0 + nsc_ref[0]) + nsh_ref[0]).astype(BF16)


def _wo_residual(mrg, w_o, x, gate, final_g=None, next_norm=None):
    b, l, d = x.shape
    tl = min(512, l)
    tok = pl.BlockSpec((1, tl, d), lambda i, j: (i, j, 0))
    per_batch = pl.BlockSpec((1, 1, d), lambda i, j: (i, 0, 0))
    vec = pl.BlockSpec((1, d), lambda i, j: (0, 0))
    in_specs = [tok, pl.BlockSpec((d, d), lambda i, j: (0, 0)), tok, per_batch]
    args = [mrg.reshape(b, l, d), w_o, x, gate]
    if final_g is not None:
        mode = "final"
        in_specs.append(vec)
        args.append(final_g.reshape(1, d))
        out_specs, out_shape = tok, jax.ShapeDtypeStruct((b, l, d), F32)
    else:
        mode = "next"
        g, sc, sh = next_norm
        in_specs += [vec, per_batch, per_batch]
        args += [g.reshape(1, d), sc, sh]
        out_specs = [tok, tok]
        out_shape = [jax.ShapeDtypeStruct((b, l, d), F32), jax.ShapeDtypeStruct((b, l, d), BF16)]
    return pl.pallas_call(
        functools.partial(_wo_kernel, mode=mode),
        grid=(b, l // tl),
        in_specs=in_specs,
        out_specs=out_specs,
        out_shape=out_shape,
        compiler_params=_cparams(("arbitrary", "arbitrary")),
        name="wo_residual_" + mode,
    )(*args)


def _mixer(h, use_rope, s0f, s0b, w_in, layer, conv_w, conv_b, kf, consts, hy_bias, ret_decay,
           w_hy_b, w_ret_b):
    b, l, d = h.shape
    h2 = h.reshape(b * l, d)
    proj = functools.partial(_proj, h2, w_in, layer)
    qk = proj(0, 2 * d, l, "rope" if use_rope else "qkscale", hw=QK_DIM)
    v = proj(2 * d, 2 * d, l, "plain", hw=V_DIM)
    rg = proj(4 * d, 2 * d, l, "silu", hw=V_DIM)
    phy = proj(6 * d, 3 * d, l, "plain")
    phg = proj(9 * d, d, l, "silu")
    pmg = proj(10 * d, 2 * d, l, "sigmoid")
    u, m = _hyena_front(phy.reshape(b, l, 3 * d), phg.reshape(b, l, d), conv_w, conv_b)
    hy = _long_conv(u, m, hy_bias, kf, layer, consts)
    ret, sf, sb = _retention(qk, d // QK_DIM, v, rg, ret_decay, s0f, s0b, b)
    mrg = _merge(hy.reshape(b * l, d), ret.reshape(b * l, 2 * d), pmg, w_hy_b, w_ret_b)
    return mrg, sf, sb


def kernel(x, c, ctx, c_ctx, ln_g, ada_w, ada_b, w_in, hy_conv_w, hy_conv_b, hy_filt_w1,
           hy_filt_b1, hy_filt_w2, hy_filt_b2, hy_filt_w3, hy_filt_b3, hy_filt_freq,
           hy_filt_wout, hy_bias, ret_decay, w_hy_out, w_ret_out, w_o, final_g):
    b, l, d = x.shape
    lc = ctx.shape[1]
    depth = ln_g.shape[0]
    h = d // QK_DIM
    assert b % 2 == 0 and d % QK_DIM == 0 and l % GRID_W == 0

    rows = -(-(b + 1) // 8) * 8
    cond = jnp.zeros((rows, d), F32).at[:b].set(c).at[b].set(c_ctx)
    mod = _ada_modulation(cond, ada_w, ada_b)

    consts_l = _dft_consts(l, FFT_N2 if (2 * l) % FFT_N2 == 0 and l >= 1024 else 1)
    consts_c = _dft_consts(lc, 1)
    zero_state = jnp.zeros((b, h, QK_DIM, V_DIM), F32)
    filt = (hy_filt_w1, hy_filt_b1, hy_filt_w2, hy_filt_b2, hy_filt_w3, hy_filt_b3, hy_filt_freq,
            hy_filt_wout)
    kf_l = _filter_spectrum(_hyena_filter_taps(l, d, *filt), consts_l)
    kf_c = _filter_spectrum(_hyena_filter_taps(lc, d, *filt), consts_c)

    lat = lambda a: a[:b, None, :]
    cx = lambda a: jnp.broadcast_to(a[b][None, None, :], (b, 1, d))
    shift = lambda i: mod[i, :, :d]
    scale = lambda i: mod[i, :, d:2 * d]
    h_ctx = _prenorm(ctx, ln_g[0], cx(scale(0)), cx(shift(0)))
    h_lat = _prenorm(x, ln_g[0], lat(scale(0)), lat(shift(0)))

    for i in range(depth):
        gt = mod[i, :, 2 * d:]
        last = i == depth - 1
        w_hy_b = w_hy_out[i].astype(BF16)
        w_ret_b = w_ret_out[i].astype(BF16)
        w_o_b = w_o[i].astype(BF16)
        params = lambda kf, consts: (w_in, i, hy_conv_w[i], hy_conv_b[i], kf, consts, hy_bias[i],
                                     ret_decay[i], w_hy_b, w_ret_b)

        if not last:
            mrg_c, s_ctx_f, s_ctx_b = _mixer(h_ctx, False, zero_state, zero_state,
                                             *params(kf_c, consts_c))
            ctx, h_ctx = _wo_residual(mrg_c, w_o_b, ctx, cx(gt),
                                      next_norm=(ln_g[i + 1], cx(scale(i + 1)), cx(shift(i + 1))))
        else:
            hc2 = h_ctx.reshape(b * lc, d)
            k_c = _proj(hc2, w_in, i, d, d, lc, "qkscale", hw=QK_DIM)
            v_c = _proj(hc2, w_in, i, 2 * d, 2 * d, lc, "plain", hw=V_DIM)
            s_ctx_f, s_ctx_b = _retention(k_c, 0, v_c, None, ret_decay[i], zero_state, zero_state,
                                          b, states_only=True)

        mrg, _, _ = _mixer(h_lat, True, s_ctx_f, s_ctx_b, *params(kf_l, consts_l))
        if last:
            x = _wo_residual(mrg, w_o_b, x, lat(gt), final_g=final_g)
        else:
            x, h_lat = _wo_residual(mrg, w_o_b, x, lat(gt),
                                    next_norm=(ln_g[i + 1], lat(scale(i + 1)), lat(shift(i + 1))))

    return x
```

```python
import functools
import math

import numpy as np
import jax
import jax.numpy as jnp
from jax import lax
from jax.experimental import pallas as pl
from jax.experimental.pallas import tpu as pltpu

F32 = jnp.float32
BF16 = jnp.bfloat16

NORM_EPS = 1e-6
GRID_W = 64
ROPE_BASE = 10000.0
QK_DIM = 256
V_DIM = 512
RET_CHUNK = 256
RET_UNROLL = 2
HY_EMB = 33
HY_BANDS = (HY_EMB - 1) // 2
HY_EMB_PAD = 64
HY_FAST_DECAY_PCT = 0.3
HY_SLOW_DECAY_PCT = 1.5
HY_DECAY_TARGET = 1e-2
FFT_N2 = 64
PROJ_NSUB = 4
LANES = 128
FFT_PITCH = 72
FFT_UNROLL = 16
VMEM_LIMIT = 56 * 1024 * 1024


def _cparams(sem):
    return pltpu.CompilerParams(dimension_semantics=sem, vmem_limit_bytes=VMEM_LIMIT)


def _silu(x):
    return x / (1.0 + jnp.exp(-x))


def _sigmoid(x):
    return 1.0 / (1.0 + jnp.exp(-x))


def _bdot(a, b):
    return jnp.dot(a, b, preferred_element_type=F32)


def _ada_kernel(c_ref, w_ref, b_ref, o_ref):
    s = _silu(c_ref[...]).astype(BF16)
    o_ref[0] = _bdot(s, w_ref[0].astype(BF16)) + b_ref[0]


def _ada_modulation(cond, ada_w, ada_b):
    depth, d, w3 = ada_w.shape
    r = cond.shape[0]
    tn = min(512, w3)
    return pl.pallas_call(
        _ada_kernel,
        grid=(depth, w3 // tn),
        in_specs=[
            pl.BlockSpec((r, d), lambda l, j: (0, 0)),
            pl.BlockSpec((1, d, tn), lambda l, j: (l, 0, j)),
            pl.BlockSpec((1, 1, tn), lambda l, j: (l, 0, j)),
        ],
        out_specs=pl.BlockSpec((1, r, tn), lambda l, j: (l, 0, j)),
        out_shape=jax.ShapeDtypeStruct((depth, r, w3), F32),
        compiler_params=_cparams(("arbitrary", "arbitrary")),
        name="ada_mod",
    )(cond, ada_w, ada_b.reshape(depth, 1, w3))


def _prenorm_kernel(x_ref, g_ref, sc_ref, sh_ref, o_ref):
    x = x_ref[0]
    y = x * lax.rsqrt(jnp.mean(x * x, axis=-1, keepdims=True) + NORM_EPS)
    o_ref[0] = ((y * g_ref[...]) * (1.0 + sc_ref[0]) + sh_ref[0]).astype(BF16)


def _prenorm(x, g, sc, sh):
    b, l, d = x.shape
    tl = min(512, l)
    return pl.pallas_call(
        _prenorm_kernel,
        grid=(b, l // tl),
        in_specs=[
            pl.BlockSpec((1, tl, d), lambda i, j: (i, j, 0)),
            pl.BlockSpec((1, d), lambda i, j: (0, 0)),
            pl.BlockSpec((1, 1, d), lambda i, j: (i, 0, 0)),
            pl.BlockSpec((1, 1, d), lambda i, j: (i, 0, 0)),
        ],
        out_specs=pl.BlockSpec((1, tl, d), lambda i, j: (i, j, 0)),
        out_shape=jax.ShapeDtypeStruct((b, l, d), BF16),
        compiler_params=_cparams(("arbitrary", "arbitrary")),
        name="prenorm",
    )(x, g.reshape(1, d), sc, sh)


def _proj_kernel(*refs, epi, nsub, tm, tn, r, hw, j0):
    if epi == "rope":
        h_ref, w_ref, cos_ref, sin_ref, o_ref, wb_scr = refs
    else:
        h_ref, w_ref, o_ref, wb_scr = refs

    @pl.when(pl.program_id(1) == 0)
    def _():
        wb_scr[...] = w_ref[...].astype(BF16)

    qk_scale = jnp.where(j0 + pl.program_id(0) < r, 1.0, QK_DIM ** -0.5).astype(F32)
    ts = tm // nsub
    sw = 128 if epi == "rope" else (hw or tn)
    for s in range(nsub):
        rows = slice(s * ts, (s + 1) * ts)
        acc = _bdot(h_ref[rows, :], wb_scr[...])
        for g in range(tn // sw):
            val = acc[:, g * sw:(g + 1) * sw]
            if epi == "rope":
                t = (g % 2) * 128
                val = (val * cos_ref[rows, t:t + 128]
                       + pltpu.roll(val, 64, 1) * sin_ref[rows, t:t + 128]) * qk_scale
            elif epi == "qkscale":
                val = val * qk_scale
            elif epi == "silu":
                val = _silu(val)
            elif epi == "sigmoid":
                val = _sigmoid(val)
            val = val.astype(BF16)
            if hw:
                c0 = g * sw
                o_ref[c0 // hw, rows, c0 % hw:c0 % hw + sw] = val
            else:
                o_ref[rows, g * sw:(g + 1) * sw] = val


def _rope_tables(l):
    quarter = QK_DIM // 4
    inv = 1.0 / (ROPE_BASE ** (jnp.arange(quarter, dtype=F32) / quarter))
    t = jnp.arange(l)
    row = (t // GRID_W).astype(F32)
    col = (t % GRID_W).astype(F32)
    ar = row[:, None] * inv[None, :]
    ac = col[:, None] * inv[None, :]
    cos_t = jnp.concatenate([jnp.cos(ar), jnp.cos(ar), jnp.cos(ac), jnp.cos(ac)], axis=-1)
    sin_t = jnp.concatenate([-jnp.sin(ar), jnp.sin(ar), -jnp.sin(ac), jnp.sin(ac)], axis=-1)
    return cos_t, sin_t


def _proj(h, w, layer, col0, ncols, l, epi, hw=None):
    m, d = h.shape
    tm = min(1024, m)
    tn = min(1024, d)
    j0 = col0 // tn
    assert epi != "rope" or l % tm == 0
    in_specs = [
        pl.BlockSpec((tm, d), lambda j, i: (i, 0)),
        pl.BlockSpec((None, d, tn), lambda j, i: (layer, 0, j0 + j)),
    ]
    args = [h, w]
    if epi == "rope":
        lb = l // tm
        in_specs += [pl.BlockSpec((tm, QK_DIM), lambda j, i: (i % lb, 0))] * 2
        args += list(_rope_tables(l))
    if hw:
        out_spec = pl.BlockSpec((tn // hw, tm, hw), lambda j, i: (j, i, 0))
        out_shape = jax.ShapeDtypeStruct((ncols // hw, m, hw), BF16)
    else:
        out_spec = pl.BlockSpec((tm, tn), lambda j, i: (i, j))
        out_shape = jax.ShapeDtypeStruct((m, ncols), BF16)
    return pl.pallas_call(
        functools.partial(_proj_kernel, epi=epi, nsub=PROJ_NSUB, tm=tm, tn=tn, r=d // tn, hw=hw,
                          j0=j0),
        grid=(ncols // tn, m // tm),
        in_specs=in_specs,
        out_specs=out_spec,
        out_shape=out_shape,
        scratch_shapes=[pltpu.VMEM((d, tn), BF16)],
        compiler_params=_cparams(("arbitrary", "arbitrary")),
        name="in_proj_" + epi,
    )(*args)


def _hyfront_kernel(v_ref, x0_ref, x1_ref, g_ref, wv_ref, w0_ref, w1_ref,
                    bv_ref, b0_ref, b1_ref, u_ref, m_ref, *, l):
    rows = lax.broadcasted_iota(jnp.int32, (l, 1), 0)

    def conv3(x_ref, w_ref, b_ref):
        x = x_ref[0].astype(F32)
        prev = jnp.where(rows == 0, 0.0, pltpu.roll(x, 1, 0))
        nxt = jnp.where(rows == l - 1, 0.0, pltpu.roll(x, l - 1, 0))
        return prev * w_ref[0:1, :] + x * w_ref[1:2, :] + nxt * w_ref[2:3, :] + b_ref[...]

    hv = conv3(v_ref, wv_ref, bv_ref)
    hx1 = conv3(x1_ref, w1_ref, b1_ref)
    u_ref[0] = (hv * hx1).astype(BF16)
    hx0 = conv3(x0_ref, w0_ref, b0_ref)
    m_ref[0] = (hx0 * g_ref[0].astype(F32)).astype(BF16)


def _hyena_front(phy, phg, conv_w, conv_b):
    b, l, d = phg.shape
    cs = 128
    nb = d // cs
    pspec = lambda off: pl.BlockSpec((1, l, cs), lambda i, j: (i, 0, off * nb + j))
    wspec = lambda off: pl.BlockSpec((3, cs), lambda i, j: (0, off * nb + j))
    bspec = lambda off: pl.BlockSpec((1, cs), lambda i, j: (0, off * nb + j))
    ospec = pl.BlockSpec((1, l, cs), lambda i, j: (i, 0, j))
    return pl.pallas_call(
        functools.partial(_hyfront_kernel, l=l),
        grid=(b, nb),
        in_specs=[pspec(0), pspec(1), pspec(2), pspec(0),
                  wspec(0), wspec(1), wspec(2), bspec(0), bspec(1), bspec(2)],
        out_specs=[ospec, ospec],
        out_shape=[jax.ShapeDtypeStruct((b, l, d), BF16)] * 2,
        compiler_params=_cparams(("arbitrary", "arbitrary")),
        name="hyena_front",
    )(phy, phy, phy, phg, conv_w, conv_w, conv_w,
      conv_b.reshape(1, -1), conv_b.reshape(1, -1), conv_b.reshape(1, -1))


def _filter_kernel(z_ref, t_ref, w1_ref, b1_ref, w2_ref, b2_ref, w3_ref, b3_ref, fr_ref,
                   wo_ref, dl_ref, o_ref, *, tr, l):
    dot = functools.partial(jnp.dot, precision=lax.Precision.HIGHEST, preferred_element_type=F32)
    f = fr_ref[...]
    h = jnp.sin(f * (dot(w1_ref[...], z_ref[...]) + b1_ref[...]))
    h = jnp.sin(f * (dot(w2_ref[...], h) + b2_ref[...]))
    h = jnp.sin(f * (dot(w3_ref[...], h) + b3_ref[...]))
    y = lax.dot_general(h.astype(BF16), wo_ref[...].astype(BF16), (((0,), (0,)), ((), ())),
                        preferred_element_type=F32)
    win = jnp.exp(-t_ref[...] * dl_ref[...])
    rows = pl.program_id(1) * tr + lax.broadcasted_iota(jnp.int32, (tr, 1), 0)
    o_ref[...] = jnp.where(rows == l, 0.0, y * win).astype(BF16)


def _hyena_filter_taps(l, c, w1, b1, w2, b2, w3, b3, freq, w_out):
    t = jnp.linspace(0.0, 1.0, l, dtype=F32)[:, None]
    ang = 2.0 * math.pi * jnp.arange(l, dtype=F32)[:, None] / l
    f = jnp.linspace(1e-4, HY_BANDS - 1, HY_BANDS, dtype=F32)[None, :]
    z = jnp.concatenate([t, jnp.cos(f * ang), -jnp.sin(f * ang)], axis=-1)
    z = jnp.pad(z, ((0, 0), (0, HY_EMB_PAD - HY_EMB)))
    back = lambda a: jnp.concatenate([a[l - 1:l], a[:0:-1]], axis=0)
    z2 = jnp.concatenate([z, back(z)], axis=0)
    t2 = jnp.concatenate([t, back(t)], axis=0)
    max_decay = math.log(HY_DECAY_TARGET) / HY_FAST_DECAY_PCT
    min_decay = math.log(HY_DECAY_TARGET) / HY_SLOW_DECAY_PCT
    deltas = jnp.abs(jnp.linspace(min_decay, max_decay, c, dtype=F32))[None, :]
    w1p = jnp.pad(w1, ((0, 0), (0, HY_EMB_PAD - HY_EMB), (0, 0)))
    hid = w2.shape[1]
    tr = min(512, l)
    lb = l // tr
    depth = w1.shape[0]
    per_layer = lambda r, cc: pl.BlockSpec((None, r, cc), lambda g, i: (g, 0, 0))
    col = lambda a: a[:, :, None]
    tp = lambda a: jnp.swapaxes(a, 1, 2)
    return pl.pallas_call(
        functools.partial(_filter_kernel, tr=tr, l=l),
        grid=(depth, 2 * lb),
        in_specs=[
            pl.BlockSpec((HY_EMB_PAD, tr), lambda g, i: (0, i)),
            pl.BlockSpec((tr, 1), lambda g, i: (i, 0)),
            per_layer(hid, HY_EMB_PAD), per_layer(hid, 1),
            per_layer(hid, hid), per_layer(hid, 1),
            per_layer(hid, hid), per_layer(hid, 1),
            per_layer(hid, 1),
            pl.BlockSpec((None, hid, c), lambda g, i: (g, 0, i // lb)),
            pl.BlockSpec((1, c), lambda g, i: (0, 0)),
        ],
        out_specs=pl.BlockSpec((None, tr, c), lambda g, i: (g, i, 0)),
        out_shape=jax.ShapeDtypeStruct((depth, 2 * l, c), BF16),
        compiler_params=_cparams(("arbitrary", "arbitrary")),
        name="hyena_filter",
    )(z2.T, t2, tp(w1p), col(b1), tp(w2), col(b2), tp(w3), col(b3), col(freq), w_out, deltas)


def _dft_consts(l, n2):
    n = 2 * l
    n1 = n // n2
    nin = n1 // 2

    def cs(k, m, period):
        ph = 2.0 * np.pi * ((np.outer(k, m)) % period) / period
        return np.cos(ph), np.sin(ph)

    k1 = np.arange(n1)
    c, s = cs(k1, np.arange(nin), n1)
    f1 = np.block([[c, s], [-s, c]])
    cf, sf = cs(k1, np.arange(n1), n1)
    f1_real = np.concatenate([cf, -sf], axis=0)
    ci, si = cs(np.arange(nin), k1, n1)
    g1 = np.block([[ci, -si], [si, ci]]) / n
    out = dict(n1=n1, n2=n2, nin=nin,
               f1=jnp.asarray(f1, F32).astype(BF16),
               f1_real=jnp.asarray(f1_real, F32).astype(BF16),
               g1=jnp.asarray(g1, F32).astype(BF16))
    if n2 > 1:
        f1_il = np.empty_like(f1)
        f1_il[0::2], f1_il[1::2] = f1[:n1], f1[n1:]
        out["f1_il"] = jnp.asarray(f1_il, F32).astype(BF16)
        out["g1_il"] = jnp.asarray(f1_il.T / n, F32).astype(BF16)
        f1r_il = np.empty_like(f1_real)
        f1r_il[0::2], f1r_il[1::2] = f1_real[:n1], f1_real[n1:]
        out["f1_real_il"] = jnp.asarray(f1r_il, F32).astype(BF16)
        m2 = np.arange(n2)
        ph = (m2[None, None, :] * k1[:, None, None] + n1 * m2[None, None, :] * m2[None, :, None]) % n
        ph = 2.0 * np.pi * ph / n
        tr, ti = np.cos(ph), -np.sin(ph)
        t_fwd = np.concatenate([np.concatenate([tr, -ti], axis=2),
                                np.concatenate([ti, tr], axis=2)], axis=1)
        t_il = np.empty_like(t_fwd)
        t_il[:, :, 0::2], t_il[:, :, 1::2] = t_fwd[:, :, :n2], t_fwd[:, :, n2:]
        out["t_il"] = jnp.asarray(t_il, F32).astype(BF16)
    return out


def _cmm_kernel(*refs, mode, chunk, nchunk):
    a_ref, x_ref, o_ref = refs[0], refs[1], refs[-1]
    a = a_ref[...]
    hm = a.shape[0] // 2
    for cc in range(nchunk):
        sl = slice(cc * chunk, (cc + 1) * chunk)
        acc = _bdot(a, x_ref[0, :, sl])
        if mode == "kfmul":
            kf_ref = refs[2]
            xr, xi = acc[:hm], acc[hm:]
            kr, ki = kf_ref[0, :, sl], kf_ref[1, :, sl]
            o_ref[0, :hm, sl] = (xr * kr - xi * ki).astype(o_ref.dtype)
            o_ref[0, hm:, sl] = (xr * ki + xi * kr).astype(o_ref.dtype)
        elif mode == "epi":
            u_ref, m_ref, b_ref = refs[2], refs[3], refs[4]
            u = u_ref[0, :, sl].astype(F32)
            o = (acc + b_ref[:, sl] * u) * m_ref[0, :, sl].astype(F32)
            o_ref[0, :, sl] = o.astype(o_ref.dtype)
        else:
            o_ref[0, :, sl] = acc.astype(o_ref.dtype)


def _cmm(a, x, mode="plain", extra=(), out_dtype=BF16):
    mr, k = a.shape
    g, _, cols = x.shape
    tc = min(8192, cols)
    chunk = min(1024, tc)
    in_specs = [pl.BlockSpec((mr, k), lambda i, j: (0, 0)),
                pl.BlockSpec((1, k, tc), lambda i, j: (i, 0, j))]
    if mode == "kfmul":
        in_specs += [pl.BlockSpec((2, mr // 2, tc), lambda i, j: (0, 0, j))]
    elif mode == "epi":
        in_specs += [pl.BlockSpec((1, mr, tc), lambda i, j: (i, 0, j)),
                     pl.BlockSpec((1, mr, tc), lambda i, j: (i, 0, j)),
                     pl.BlockSpec((1, tc), lambda i, j: (0, j))]
    return pl.pallas_call(
        functools.partial(_cmm_kernel, mode=mode, chunk=chunk, nchunk=tc // chunk),
        grid=(g, cols // tc),
        in_specs=in_specs,
        out_specs=pl.BlockSpec((1, mr, tc), lambda i, j: (i, 0, j)),
        out_shape=jax.ShapeDtypeStruct((g, mr, cols), out_dtype),
        compiler_params=_cparams(("arbitrary", "arbitrary")),
        name="dft_" + mode,
    )(a, x, *extra)


def _filter_fft_kernel(x_ref, f1_ref, t_ref, o_ref, s1, ab, *, n1, n2, nf):
    lanes = lambda f: jnp.concatenate([f(s) for s in range(nf)], axis=1)

    for s in range(nf):
        def fill(i, carry, s=s):
            src = pl.ds(pl.multiple_of(i * n2, n2), n2)
            s1[s, pl.ds(pl.multiple_of(i * FFT_PITCH, 8), n2), :] = x_ref[s, src, :].astype(F32)
            return carry
        lax.fori_loop(0, n1, fill, 0, unroll=8)

    def stage1(j, carry):
        rows = pl.ds(j, n1, stride=FFT_PITCH)
        z = lanes(lambda s: s1[s, rows, :]).astype(BF16)
        a = pltpu.bitcast(_bdot(f1_ref[...], z).astype(BF16), jnp.uint32)
        for s in range(nf):
            ab[s, rows, :] = a[:, s * LANES:(s + 1) * LANES]
        return carry

    lax.fori_loop(0, n2, stage1, 0, unroll=FFT_UNROLL)

    def mid(k, carry):
        rows = pl.ds(pl.multiple_of(k * FFT_PITCH, 8), n2)
        a = pltpu.bitcast(lanes(lambda s: ab[s, rows, :]), BF16)
        x = _bdot(t_ref[k], a).astype(BF16)
        for s in range(nf):
            o_ref[s, 0, k] = x[:n2, s * LANES:(s + 1) * LANES]
            o_ref[s, 1, k] = x[n2:, s * LANES:(s + 1) * LANES]
        return carry

    lax.fori_loop(0, n1, mid, 0, unroll=FFT_UNROLL)


def _filter_spectrum(taps, consts):
    nf, n, c = taps.shape
    n1, n2 = consts["n1"], consts["n2"]
    if n2 == 1:
        return _cmm(consts["f1_real"], taps, out_dtype=F32).reshape(nf, 2, n1, c)
    const = lambda shape: pl.BlockSpec(shape, lambda j: (0,) * len(shape))
    return pl.pallas_call(
        functools.partial(_filter_fft_kernel, n1=n1, n2=n2, nf=nf),
        grid=(c // LANES,),
        in_specs=[pl.BlockSpec((nf, n, LANES), lambda j: (0, 0, j)),
                  const((2 * n1, n1)), const((n1, 2 * n2, 2 * n2))],
        out_specs=pl.BlockSpec((nf, 2, n1, n2, LANES), lambda j: (0, 0, 0, 0, j)),
        out_shape=jax.ShapeDtypeStruct((nf, 2, n1, n2, c), BF16),
        scratch_shapes=[pltpu.VMEM((nf, n1 * FFT_PITCH, LANES), F32),
                        pltpu.VMEM((nf, n1 * FFT_PITCH, LANES), jnp.uint32)],
        compiler_params=_cparams(("arbitrary",)),
        name="filter_fft",
    )(taps, consts["f1_real_il"], consts["t_il"])


def _conv_kernel(u_ref, m_ref, hb_ref, kf_ref, f1_ref, t_ref, g1_ref, o_ref, s1, ab, *, n1, n2, nb):
    nin = n1 // 2
    npair = nb // 2
    lanes = lambda f: jnp.concatenate([f(s) for s in range(npair)], axis=1)

    for bi in range(nb):
        def fill(i, carry, bi=bi):
            src = pl.ds(pl.multiple_of(i * n2, n2), n2)
            dst = pl.ds(pl.multiple_of(i * FFT_PITCH, 8), n2)
            s1[bi % 2, bi // 2, dst, :] = u_ref[bi, src, :].astype(F32)
            return carry
        lax.fori_loop(0, nin, fill, 0, unroll=8)

    def stage1(j, carry):
        rows = pl.ds(j, nin, stride=FFT_PITCH)
        z = jnp.concatenate([lanes(lambda s: s1[0, s, rows, :]),
                             lanes(lambda s: s1[1, s, rows, :])], axis=0).astype(BF16)
        a = pltpu.bitcast(_bdot(f1_ref[...], z).astype(BF16), jnp.uint32)
        for s in range(npair):
            ab[s, pl.ds(j, n1, stride=FFT_PITCH), :] = a[:, s * LANES:(s + 1) * LANES]
        return carry

    lax.fori_loop(0, n2, stage1, 0, unroll=FFT_UNROLL)

    def mid(k, carry):
        rows = pl.ds(pl.multiple_of(k * FFT_PITCH, 8), n2)
        a = pltpu.bitcast(lanes(lambda s: ab[s, rows, :]), BF16)
        t = t_ref[k]
        x = _bdot(t, a)
        xr, xi = x[:n2], x[n2:]
        kr = lanes(lambda s: kf_ref[0, k].astype(F32))
        ki = lanes(lambda s: kf_ref[1, k].astype(F32))
        y = jnp.concatenate([xr * kr - xi * ki, xr * ki + xi * kr], axis=0).astype(BF16)
        bm = lax.dot_general(t, y, (((0,), (0,)), ((), ())), preferred_element_type=F32)
        w = pltpu.bitcast(bm.astype(BF16), jnp.uint32)
        for s in range(npair):
            ab[s, rows, :] = w[:, s * LANES:(s + 1) * LANES]
        return carry

    lax.fori_loop(0, n1, mid, 0, unroll=FFT_UNROLL)

    def stage4(j, carry):
        b = pltpu.bitcast(lanes(lambda s: ab[s, pl.ds(j, n1, stride=FFT_PITCH), :]), BF16)
        y = _bdot(g1_ref[...], b)
        rows = pl.ds(j, nin, stride=FFT_PITCH)
        for s in range(npair):
            s1[0, s, rows, :] = y[:nin, s * LANES:(s + 1) * LANES]
            s1[1, s, rows, :] = y[nin:, s * LANES:(s + 1) * LANES]
        return carry

    lax.fori_loop(0, n2, stage4, 0, unroll=FFT_UNROLL)

    for bi in range(nb):
        def emit(i, carry, bi=bi):
            dst = pl.ds(pl.multiple_of(i * n2, n2), n2)
            src = pl.ds(pl.multiple_of(i * FFT_PITCH, 8), n2)
            y = s1[bi % 2, bi // 2, src, :] + hb_ref[...] * u_ref[bi, dst, :].astype(F32)
            o_ref[bi, dst, :] = (y * m_ref[bi, dst, :].astype(F32)).astype(BF16)
            return carry
        lax.fori_loop(0, nin, emit, 0, unroll=8)


def _long_conv(u, m, bias, kf, layer, consts):
    b, l, c = u.shape
    n1, n2, nin = consts["n1"], consts["n2"], consts["nin"]
    p = b // 2
    if n2 == 1:
        x = u.reshape(p, 2 * nin, c)
        y = _cmm(consts["f1"], x, mode="kfmul", extra=(kf[layer],))
        extra = (x, m.reshape(p, 2 * nin, c), bias.reshape(1, c))
        return _cmm(consts["g1"], y, mode="epi", extra=extra).reshape(b, l, c)
    const = lambda shape: pl.BlockSpec(shape, lambda j: (0,) * len(shape))
    return pl.pallas_call(
        functools.partial(_conv_kernel, n1=n1, n2=n2, nb=b),
        grid=(c // LANES,),
        in_specs=[pl.BlockSpec((b, l, LANES), lambda j: (0, 0, j)),
                  pl.BlockSpec((b, l, LANES), lambda j: (0, 0, j)),
                  pl.BlockSpec((1, LANES), lambda j: (0, j)),
                  pl.BlockSpec((None, 2, n1, n2, LANES), lambda j: (layer, 0, 0, 0, j)),
                  const((2 * n1, n1)), const((n1, 2 * n2, 2 * n2)), const((n1, 2 * n1))],
        out_specs=pl.BlockSpec((b, l, LANES), lambda j: (0, 0, j)),
        out_shape=jax.ShapeDtypeStruct((b, l, c), BF16),
        scratch_shapes=[pltpu.VMEM((2, p, nin * FFT_PITCH, LANES), F32),
                        pltpu.VMEM((p, n1 * FFT_PITCH, LANES), jnp.uint32)],
        compiler_params=_cparams(("arbitrary",)),
        name="long_conv",
    )(u, m, bias.reshape(1, c), kf, consts["f1_il"], consts["t_il"], consts["g1_il"])


def _ret_kernel(*refs, nch, cl, states_only):
    if states_only:
        rd_ref, k_ref, v_ref, s0f_ref, s0b_ref, sf_ref, sb_ref, sf_scr, sb_scr = refs
    else:
        (rd_ref, q_ref, k_ref, v_ref, g_ref, s0f_ref, s0b_ref,
         o_ref, sf_ref, sb_ref, sf_scr, sb_scr, sfall_scr, sball_scr, dmask_scr) = refs
    hd = pl.program_id(1)
    one = jnp.ones((1, 1), F32)
    lgf = -jnp.exp(one * rd_ref[0, hd])
    lgb = -jnp.exp(one * rd_ref[1, hd])
    ri = lax.broadcasted_iota(jnp.int32, (cl, 1), 0).astype(F32)
    ci = lax.broadcasted_iota(jnp.int32, (1, cl), 1).astype(F32)
    per_token = lambda dec: jnp.broadcast_to(dec, (cl, QK_DIM)).astype(BF16)
    k_dec_f = per_token(jnp.exp(lgf * (cl - 1.0 - ri)))
    k_dec_b = per_token(jnp.exp(lgb * ri))
    chunk_dec_f = jnp.exp(lgf * cl)
    chunk_dec_b = jnp.exp(lgb * cl)

    def kv_outer(k, dec, v):
        return lax.dot_general(k * dec, v, (((0,), (0,)), ((), ())), preferred_element_type=F32)

    def load(ref, c):
        return ref[0, 0, pl.ds(pl.multiple_of(c * cl, cl), cl), :]

    sf_scr[...] = s0f_ref[0, 0]
    sb_scr[...] = s0b_ref[0, 0]

    def sweep(t, carry):
        cb = nch - 1 - t
        if not states_only:
            sfall_scr[t] = sf_scr[...].astype(BF16)
            sball_scr[cb] = sb_scr[...].astype(BF16)
        sf_scr[...] = (sf_scr[...] * chunk_dec_f
                       + kv_outer(load(k_ref, t), k_dec_f, load(v_ref, t)))
        sb_scr[...] = (sb_scr[...] * chunk_dec_b
                       + kv_outer(load(k_ref, cb), k_dec_b, load(v_ref, cb)))
        return carry

    lax.fori_loop(0, nch, sweep, 0, unroll=2 * RET_UNROLL)
    sf_ref[0, 0] = sf_scr[...]
    sb_ref[0, 0] = sb_scr[...]
    if states_only:
        return

    q_dec_f = per_token(jnp.exp(lgf * (ri + 1.0)))
    q_dec_b = per_token(jnp.exp(lgb * (cl - ri)))
    diff = ri - ci
    dmask_scr[...] = (jnp.where(diff >= 0, jnp.exp(lgf * jnp.maximum(diff, 0.0)), 0.0)
                      + jnp.where(diff <= 0, jnp.exp(lgb * jnp.maximum(-diff, 0.0)), 0.0))

    def emit(c, carry):
        q = load(q_ref, c)
        k = load(k_ref, c)
        v = load(v_ref, c)
        scores = lax.dot_general(q, k, (((1,), (1,)), ((), ())), preferred_element_type=F32)
        o = _bdot((scores * dmask_scr[...]).astype(BF16), v)
        q2 = jnp.concatenate([q * q_dec_f, q * q_dec_b], axis=1)
        s2 = jnp.concatenate([sfall_scr[c], sball_scr[c]], axis=0)
        o = o + _bdot(q2, s2)
        o = o * lax.rsqrt(jnp.mean(o * o, axis=-1, keepdims=True) + NORM_EPS)
        r0 = pl.multiple_of(c * cl, cl)
        o_ref[0, pl.ds(r0, cl), :] = o.astype(BF16) * load(g_ref, c)
        return carry

    lax.fori_loop(0, nch, emit, 0, unroll=2 * RET_UNROLL)


def _retention(qk, k_off, v, gate, ret_decay, s0f, s0b, b, states_only=False):
    h = v.shape[0]
    l = v.shape[1] // b
    cl = min(RET_CHUNK, l)
    nch = l // cl
    sspec = pl.BlockSpec((1, 1, QK_DIM, V_DIM), lambda i, j: (i, j, 0, 0))
    state_shape = jax.ShapeDtypeStruct((b, h, QK_DIM, V_DIM), F32)
    hspec = lambda w, off: pl.BlockSpec((1, 1, l, w), lambda i, j: (off + j, i, 0, 0))
    qk4 = qk.reshape(k_off + h, b, l, QK_DIM)
    v4 = v.reshape(h, b, l, V_DIM)
    smem = pl.BlockSpec(memory_space=pltpu.SMEM)
    kern = functools.partial(_ret_kernel, nch=nch, cl=cl, states_only=states_only)
    if states_only:
        return pl.pallas_call(
            kern, grid=(b, h),
            in_specs=[smem, hspec(QK_DIM, k_off), hspec(V_DIM, 0), sspec, sspec],
            out_specs=[sspec, sspec],
            out_shape=[state_shape, state_shape],
            scratch_shapes=[pltpu.VMEM((QK_DIM, V_DIM), F32)] * 2,
            compiler_params=_cparams(("arbitrary", "arbitrary")),
            name="retention_states",
        )(ret_decay, qk4, v4, s0f, s0b)
    return pl.pallas_call(
        kern, grid=(b, h),
        in_specs=[smem, hspec(QK_DIM, 0), hspec(QK_DIM, k_off), hspec(V_DIM, 0), hspec(V_DIM, 0),
                  sspec, sspec],
        out_specs=[pl.BlockSpec((1, l, V_DIM), lambda i, j: (i, 0, j)), sspec, sspec],
        out_shape=[jax.ShapeDtypeStruct((b, l, h * V_DIM), BF16), state_shape, state_shape],
        scratch_shapes=[pltpu.VMEM((QK_DIM, V_DIM), F32),
                        pltpu.VMEM((QK_DIM, V_DIM), F32),
                        pltpu.VMEM((nch, QK_DIM, V_DIM), BF16),
                        pltpu.VMEM((nch, QK_DIM, V_DIM), BF16),
                        pltpu.VMEM((cl, cl), F32)],
        compiler_params=_cparams(("arbitrary", "arbitrary")),
        name="retention",
    )(ret_decay, qk4, qk4, v4, gate.reshape(h, b, l, V_DIM), s0f, s0b)


def _merge_kernel(hy_ref, ret_ref, why_ref, wret_ref, ghy_ref, gret_ref, o_ref):
    a = _bdot(hy_ref[...], why_ref[...])
    b = _bdot(ret_ref[...], wret_ref[...])
    o_ref[...] = (ghy_ref[...].astype(F32) * a + gret_ref[...].astype(F32) * b).astype(BF16)


def _merge(hy, ret, pmg, w_hy, w_ret):
    mm, d = hy.shape
    tm = min(512, mm)
    tn = min(1024, d)
    nb = d // tn
    return pl.pallas_call(
        _merge_kernel,
        grid=(mm // tm, nb),
        in_specs=[
            pl.BlockSpec((tm, d), lambda i, j: (i, 0)),
            pl.BlockSpec((tm, 2 * d), lambda i, j: (i, 0)),
            pl.BlockSpec((d, tn), lambda i, j: (0, j)),
            pl.BlockSpec((2 * d, tn), lambda i, j: (0, j)),
            pl.BlockSpec((tm, tn), lambda i, j: (i, j)),
            pl.BlockSpec((tm, tn), lambda i, j: (i, nb + j)),
        ],
        out_specs=pl.BlockSpec((tm, tn), lambda i, j: (i, j)),
        out_shape=jax.ShapeDtypeStruct((mm, d), BF16),
        compiler_params=_cparams(("arbitrary", "arbitrary")),
        name="merge_proj",
    )(hy, ret, w_hy, w_ret, pmg, pmg)


def _wo_kernel(*refs, mode):
    if mode == "final":
        m_ref, w_ref, x_ref, g_ref, fg_ref, o_ref = refs
    else:
        m_ref, w_ref, x_ref, g_ref, ng_ref, nsc_ref, nsh_ref, o_ref, h_ref = refs
    xn = x_ref[0] + g_ref[0] * _bdot(m_ref[0], w_ref[...])
    y = xn * lax.rsqrt(jnp.mean(xn * xn, axis=-1, keepdims=True) + NORM_EPS)
    if mode == "final":
        o_ref[0] = y * fg_ref[...]
    else:
        o_ref[0] = xn
        h_ref[0] = ((y * ng_ref[...]) * (1.0 + nsc_ref[0]) + nsh_ref[0]).astype(BF16)


def _wo_residual(mrg, w_o, x, gate, final_g=None, next_norm=None):
    b, l, d = x.shape
    tl = min(512, l)
    tok = pl.BlockSpec((1, tl, d), lambda i, j: (i, j, 0))
    per_batch = pl.BlockSpec((1, 1, d), lambda i, j: (i, 0, 0))
    vec = pl.BlockSpec((1, d), lambda i, j: (0, 0))
    in_specs = [tok, pl.BlockSpec((d, d), lambda i, j: (0, 0)), tok, per_batch]
    args = [mrg.reshape(b, l, d), w_o, x, gate]
    if final_g is not None:
        mode = "final"
        in_specs.append(vec)
        args.append(final_g.reshape(1, d))
        out_specs, out_shape = tok, jax.ShapeDtypeStruct((b, l, d), F32)
    else:
        mode = "next"
        g, sc, sh = next_norm
        in_specs += [vec, per_batch, per_batch]
        args += [g.reshape(1, d), sc, sh]
        out_specs = [tok, tok]
        out_shape = [jax.ShapeDtypeStruct((b, l, d), F32), jax.ShapeDtypeStruct((b, l, d), BF16)]
    return pl.pallas_call(
        functools.partial(_wo_kernel, mode=mode),
        grid=(b, l // tl),
        in_specs=in_specs,
        out_specs=out_specs,
        out_shape=out_shape,
        compiler_params=_cparams(("arbitrary", "arbitrary")),
        name="wo_residual_" + mode,
    )(*args)


def _mixer(h, use_rope, s0f, s0b, w_in, layer, conv_w, conv_b, kf, consts, hy_bias, ret_decay,
           w_hy_b, w_ret_b):
    b, l, d = h.shape
    h2 = h.reshape(b * l, d)
    proj = functools.partial(_proj, h2, w_in, layer)
    qk = proj(0, 2 * d, l, "rope" if use_rope else "qkscale", hw=QK_DIM)
    v = proj(2 * d, 2 * d, l, "plain", hw=V_DIM)
    rg = proj(4 * d, 2 * d, l, "silu", hw=V_DIM)
    phy = proj(6 * d, 3 * d, l, "plain")
    phg = proj(9 * d, d, l, "silu")
    pmg = proj(10 * d, 2 * d, l, "sigmoid")
    u, m = _hyena_front(phy.reshape(b, l, 3 * d), phg.reshape(b, l, d), conv_w, conv_b)
    hy = _long_conv(u, m, hy_bias, kf, layer, consts)
    ret, sf, sb = _retention(qk, d // QK_DIM, v, rg, ret_decay, s0f, s0b, b)
    mrg = _merge(hy.reshape(b * l, d), ret.reshape(b * l, 2 * d), pmg, w_hy_b, w_ret_b)
    return mrg, sf, sb


def kernel(x, c, ctx, c_ctx, ln_g, ada_w, ada_b, w_in, hy_conv_w, hy_conv_b, hy_filt_w1,
           hy_filt_b1, hy_filt_w2, hy_filt_b2, hy_filt_w3, hy_filt_b3, hy_filt_freq,
           hy_filt_wout, hy_bias, ret_decay, w_hy_out, w_ret_out, w_o, final_g):
    b, l, d = x.shape
    lc = ctx.shape[1]
    depth = ln_g.shape[0]
    h = d // QK_DIM
    assert b % 2 == 0 and d % QK_DIM == 0 and l % GRID_W == 0

    rows = -(-(b + 1) // 8) * 8
    cond = jnp.zeros((rows, d), F32).at[:b].set(c).at[b].set(c_ctx)
    mod = _ada_modulation(cond, ada_w, ada_b)

    consts_l = _dft_consts(l, FFT_N2 if (2 * l) % FFT_N2 == 0 and l >= 1024 else 1)
    consts_c = _dft_consts(lc, 1)
    zero_state = jnp.zeros((b, h, QK_DIM, V_DIM), F32)
    filt = (hy_filt_w1, hy_filt_b1, hy_filt_w2, hy_filt_b2, hy_filt_w3, hy_filt_b3, hy_filt_freq,
            hy_filt_wout)
    kf_l = _filter_spectrum(_hyena_filter_taps(l, d, *filt), consts_l)
    kf_c = _filter_spectrum(_hyena_filter_taps(lc, d, *filt), consts_c)

    lat = lambda a: a[:b, None, :]
    cx = lambda a: jnp.broadcast_to(a[b][None, None, :], (b, 1, d))
    shift = lambda i: mod[i, :, :d]
    scale = lambda i: mod[i, :, d:2 * d]
    h_ctx = _prenorm(ctx, ln_g[0], cx(scale(0)), cx(shift(0)))
    h_lat = _prenorm(x, ln_g[0], lat(scale(0)), lat(shift(0)))

    for i in range(depth):
        gt = mod[i, :, 2 * d:]
        last = i == depth - 1
        w_hy_b = w_hy_out[i].astype(BF16)
        w_ret_b = w_ret_out[i].astype(BF16)
        w_o_b = w_o[i].astype(BF16)
        params = lambda kf, consts: (w_in, i, hy_conv_w[i], hy_conv_b[i], kf, consts, hy_bias[i],
                                     ret_decay[i], w_hy_b, w_ret_b)

        if not last:
            mrg_c, s_ctx_f, s_ctx_b = _mixer(h_ctx, False, zero_state, zero_state,
                                             *params(kf_c, consts_c))
            ctx, h_ctx = _wo_residual(mrg_c, w_o_b, ctx, cx(gt),
                                      next_norm=(ln_g[i + 1], cx(scale(i + 1)), cx(shift(i + 1))))
        else:
            hc2 = h_ctx.reshape(b * lc, d)
            k_c = _proj(hc2, w_in, i, d, d, lc, "qkscale", hw=QK_DIM)
            v_c = _proj(hc2, w_in, i, 2 * d, 2 * d, lc, "plain", hw=V_DIM)
            s_ctx_f, s_ctx_b = _retention(k_c, 0, v_c, None, ret_decay[i], zero_state, zero_state,
                                          b, states_only=True)

        mrg, _, _ = _mixer(h_lat, True, s_ctx_f, s_ctx_b, *params(kf_l, consts_l))
        if last:
            x = _wo_residual(mrg, w_o_b, x, lat(gt), final_g=final_g)
        else:
            x, h_lat = _wo_residual(mrg, w_o_b, x, lat(gt),
                                    next_norm=(ln_g[i + 1], lat(scale(i + 1)), lat(shift(i + 1))))

    return x
```

```python
import functools
import math

import numpy as np
import jax
import jax.numpy as jnp
from jax import lax
from jax.experimental import pallas as pl
from jax.experimental.pallas import tpu as pltpu

F32 = jnp.float32
BF16 = jnp.bfloat16

NORM_EPS = 1e-6
GRID_W = 64
ROPE_BASE = 10000.0
QK_DIM = 256
V_DIM = 512
RET_CHUNK = 256
RET_UNROLL = 4
HY_EMB = 33
HY_BANDS = (HY_EMB - 1) // 2
HY_EMB_PAD = 64
HY_FAST_DECAY_PCT = 0.3
HY_SLOW_DECAY_PCT = 1.5
HY_DECAY_TARGET = 1e-2
FFT_N2 = 64
PROJ_NSUB = 4
LANES = 128
FFT_PITCH = 72
FFT_UNROLL = 16
VMEM_LIMIT = 56 * 1024 * 1024


def _cparams(sem):
    return pltpu.CompilerParams(dimension_semantics=sem, vmem_limit_bytes=VMEM_LIMIT)


def _silu(x):
    return x / (1.0 + jnp.exp(-x))


def _sigmoid(x):
    return 1.0 / (1.0 + jnp.exp(-x))


def _bdot(a, b):
    return jnp.dot(a, b, preferred_element_type=F32)


def _ada_kernel(c_ref, w_ref, b_ref, o_ref):
    s = _silu(c_ref[...]).astype(BF16)
    o_ref[0] = _bdot(s, w_ref[0].astype(BF16)) + b_ref[0]


def _ada_modulation(cond, ada_w, ada_b):
    depth, d, w3 = ada_w.shape
    r = cond.shape[0]
    tn = min(512, w3)
    return pl.pallas_call(
        _ada_kernel,
        grid=(depth, w3 // tn),
        in_specs=[
            pl.BlockSpec((r, d), lambda l, j: (0, 0)),
            pl.BlockSpec((1, d, tn), lambda l, j: (l, 0, j)),
            pl.BlockSpec((1, 1, tn), lambda l, j: (l, 0, j)),
        ],
        out_specs=pl.BlockSpec((1, r, tn), lambda l, j: (l, 0, j)),
        out_shape=jax.ShapeDtypeStruct((depth, r, w3), F32),
        compiler_params=_cparams(("arbitrary", "arbitrary")),
        name="ada_mod",
    )(cond, ada_w, ada_b.reshape(depth, 1, w3))


def _prenorm_kernel(x_ref, g_ref, sc_ref, sh_ref, o_ref):
    x = x_ref[0]
    y = x * lax.rsqrt(jnp.mean(x * x, axis=-1, keepdims=True) + NORM_EPS)
    o_ref[0] = ((y * g_ref[...]) * (1.0 + sc_ref[0]) + sh_ref[0]).astype(BF16)


def _prenorm(x, g, sc, sh):
    b, l, d = x.shape
    tl = min(512, l)
    return pl.pallas_call(
        _prenorm_kernel,
        grid=(b, l // tl),
        in_specs=[
            pl.BlockSpec((1, tl, d), lambda i, j: (i, j, 0)),
            pl.BlockSpec((1, d), lambda i, j: (0, 0)),
            pl.BlockSpec((1, 1, d), lambda i, j: (i, 0, 0)),
            pl.BlockSpec((1, 1, d), lambda i, j: (i, 0, 0)),
        ],
        out_specs=pl.BlockSpec((1, tl, d), lambda i, j: (i, j, 0)),
        out_shape=jax.ShapeDtypeStruct((b, l, d), BF16),
        compiler_params=_cparams(("arbitrary", "arbitrary")),
        name="prenorm",
    )(x, g.reshape(1, d), sc, sh)


def _proj_kernel(*refs, epi, nsub, tm, tn, r, hw, j0):
    if epi == "rope":
        h_ref, w_ref, cos_ref, sin_ref, o_ref, wb_scr = refs
    else:
        h_ref, w_ref, o_ref, wb_scr = refs

    @pl.when(pl.program_id(1) == 0)
    def _():
        wb_scr[...] = w_ref[...].astype(BF16)

    qk_scale = jnp.where(j0 + pl.program_id(0) < r, 1.0, QK_DIM ** -0.5).astype(F32)
    ts = tm // nsub
    sw = 128 if epi == "rope" else (hw or tn)
    for s in range(nsub):
        rows = slice(s * ts, (s + 1) * ts)
        acc = _bdot(h_ref[rows, :], wb_scr[...])
        for g in range(tn // sw):
            val = acc[:, g * sw:(g + 1) * sw]
            if epi == "rope":
                t = (g % 2) * 128
                val = (val * cos_ref[rows, t:t + 128]
                       + pltpu.roll(val, 64, 1) * sin_ref[rows, t:t + 128]) * qk_scale
            elif epi == "qkscale":
                val = val * qk_scale
            elif epi == "silu":
                val = _silu(val)
            elif epi == "sigmoid":
                val = _sigmoid(val)
            val = val.astype(BF16)
            if hw:
                c0 = g * sw
                o_ref[c0 // hw, rows, c0 % hw:c0 % hw + sw] = val
            else:
                o_ref[rows, g * sw:(g + 1) * sw] = val


def _rope_tables(l):
    quarter = QK_DIM // 4
    inv = 1.0 / (ROPE_BASE ** (jnp.arange(quarter, dtype=F32) / quarter))
    t = jnp.arange(l)
    row = (t // GRID_W).astype(F32)
    col = (t % GRID_W).astype(F32)
    ar = row[:, None] * inv[None, :]
    ac = col[:, None] * inv[None, :]
    cos_t = jnp.concatenate([jnp.cos(ar), jnp.cos(ar), jnp.cos(ac), jnp.cos(ac)], axis=-1)
    sin_t = jnp.concatenate([-jnp.sin(ar), jnp.sin(ar), -jnp.sin(ac), jnp.sin(ac)], axis=-1)
    return cos_t, sin_t


def _proj(h, w, layer, col0, ncols, l, epi, hw=None):
    m, d = h.shape
    tm = min(1024, m)
    tn = min(1024, d)
    j0 = col0 // tn
    assert epi != "rope" or l % tm == 0
    in_specs = [
        pl.BlockSpec((tm, d), lambda j, i: (i, 0)),
        pl.BlockSpec((None, d, tn), lambda j, i: (layer, 0, j0 + j)),
    ]
    args = [h, w]
    if epi == "rope":
        lb = l // tm
        in_specs += [pl.BlockSpec((tm, QK_DIM), lambda j, i: (i % lb, 0))] * 2
        args += list(_rope_tables(l))
    if hw:
        out_spec = pl.BlockSpec((tn // hw, tm, hw), lambda j, i: (j, i, 0))
        out_shape = jax.ShapeDtypeStruct((ncols // hw, m, hw), BF16)
    else:
        out_spec = pl.BlockSpec((tm, tn), lambda j, i: (i, j))
        out_shape = jax.ShapeDtypeStruct((m, ncols), BF16)
    return pl.pallas_call(
        functools.partial(_proj_kernel, epi=epi, nsub=PROJ_NSUB, tm=tm, tn=tn, r=d // tn, hw=hw,
                          j0=j0),
        grid=(ncols // tn, m // tm),
        in_specs=in_specs,
        out_specs=out_spec,
        out_shape=out_shape,
        scratch_shapes=[pltpu.VMEM((d, tn), BF16)],
        compiler_params=_cparams(("arbitrary", "arbitrary")),
        name="in_proj_" + epi,
    )(*args)


def _hyfront_kernel(v_ref, x0_ref, x1_ref, g_ref, wv_ref, w0_ref, w1_ref,
                    bv_ref, b0_ref, b1_ref, u_ref, m_ref, *, l):
    rows = lax.broadcasted_iota(jnp.int32, (l, 1), 0)

    def conv3(x_ref, w_ref, b_ref):
        x = x_ref[0].astype(F32)
        prev = jnp.where(rows == 0, 0.0, pltpu.roll(x, 1, 0))
        nxt = jnp.where(rows == l - 1, 0.0, pltpu.roll(x, l - 1, 0))
        return prev * w_ref[0:1, :] + x * w_ref[1:2, :] + nxt * w_ref[2:3, :] + b_ref[...]

    hv = conv3(v_ref, wv_ref, bv_ref)
    hx1 = conv3(x1_ref, w1_ref, b1_ref)
    u_ref[0] = (hv * hx1).astype(BF16)
    hx0 = conv3(x0_ref, w0_ref, b0_ref)
    m_ref[0] = (hx0 * g_ref[0].astype(F32)).astype(BF16)


def _hyena_front(phy, phg, conv_w, conv_b):
    b, l, d = phg.shape
    cs = 128
    nb = d // cs
    pspec = lambda off: pl.BlockSpec((1, l, cs), lambda i, j: (i, 0, off * nb + j))
    wspec = lambda off: pl.BlockSpec((3, cs), lambda i, j: (0, off * nb + j))
    bspec = lambda off: pl.BlockSpec((1, cs), lambda i, j: (0, off * nb + j))
    ospec = pl.BlockSpec((1, l, cs), lambda i, j: (i, 0, j))
    return pl.pallas_call(
        functools.partial(_hyfront_kernel, l=l),
        grid=(b, nb),
        in_specs=[pspec(0), pspec(1), pspec(2), pspec(0),
                  wspec(0), wspec(1), wspec(2), bspec(0), bspec(1), bspec(2)],
        out_specs=[ospec, ospec],
        out_shape=[jax.ShapeDtypeStruct((b, l, d), BF16)] * 2,
        compiler_params=_cparams(("arbitrary", "arbitrary")),
        name="hyena_front",
    )(phy, phy, phy, phg, conv_w, conv_w, conv_w,
      conv_b.reshape(1, -1), conv_b.reshape(1, -1), conv_b.reshape(1, -1))


def _filter_kernel(z_ref, t_ref, w1_ref, b1_ref, w2_ref, b2_ref, w3_ref, b3_ref, fr_ref,
                   wo_ref, dl_ref, o_ref, *, tr, l):
    dot = functools.partial(jnp.dot, precision=lax.Precision.HIGHEST, preferred_element_type=F32)
    f = fr_ref[...]
    h = jnp.sin(f * (dot(w1_ref[...], z_ref[...]) + b1_ref[...]))
    h = jnp.sin(f * (dot(w2_ref[...], h) + b2_ref[...]))
    h = jnp.sin(f * (dot(w3_ref[...], h) + b3_ref[...]))
    y = lax.dot_general(h.astype(BF16), wo_ref[...].astype(BF16), (((0,), (0,)), ((), ())),
                        preferred_element_type=F32)
    win = jnp.exp(-t_ref[...] * dl_ref[...])
    rows = pl.program_id(1) * tr + lax.broadcasted_iota(jnp.int32, (tr, 1), 0)
    o_ref[...] = jnp.where(rows == l, 0.0, y * win).astype(BF16)


def _hyena_filter_taps(l, c, w1, b1, w2, b2, w3, b3, freq, w_out):
    t = jnp.linspace(0.0, 1.0, l, dtype=F32)[:, None]
    ang = 2.0 * math.pi * jnp.arange(l, dtype=F32)[:, None] / l
    f = jnp.linspace(1e-4, HY_BANDS - 1, HY_BANDS, dtype=F32)[None, :]
    z = jnp.concatenate([t, jnp.cos(f * ang), -jnp.sin(f * ang)], axis=-1)
    z = jnp.pad(z, ((0, 0), (0, HY_EMB_PAD - HY_EMB)))
    back = lambda a: jnp.concatenate([a[l - 1:l], a[:0:-1]], axis=0)
    z2 = jnp.concatenate([z, back(z)], axis=0)
    t2 = jnp.concatenate([t, back(t)], axis=0)
    max_decay = math.log(HY_DECAY_TARGET) / HY_FAST_DECAY_PCT
    min_decay = math.log(HY_DECAY_TARGET) / HY_SLOW_DECAY_PCT
    deltas = jnp.abs(jnp.linspace(min_decay, max_decay, c, dtype=F32))[None, :]
    w1p = jnp.pad(w1, ((0, 0), (0, HY_EMB_PAD - HY_EMB), (0, 0)))
    hid = w2.shape[1]
    tr = min(512, l)
    lb = l // tr
    depth = w1.shape[0]
    per_layer = lambda r, cc: pl.BlockSpec((None, r, cc), lambda g, i: (g, 0, 0))
    col = lambda a: a[:, :, None]
    tp = lambda a: jnp.swapaxes(a, 1, 2)
    return pl.pallas_call(
        functools.partial(_filter_kernel, tr=tr, l=l),
        grid=(depth, 2 * lb),
        in_specs=[
            pl.BlockSpec((HY_EMB_PAD, tr), lambda g, i: (0, i)),
            pl.BlockSpec((tr, 1), lambda g, i: (i, 0)),
            per_layer(hid, HY_EMB_PAD), per_layer(hid, 1),
            per_layer(hid, hid), per_layer(hid, 1),
            per_layer(hid, hid), per_layer(hid, 1),
            per_layer(hid, 1),
            pl.BlockSpec((None, hid, c), lambda g, i: (g, 0, i // lb)),
            pl.BlockSpec((1, c), lambda g, i: (0, 0)),
        ],
        out_specs=pl.BlockSpec((None, tr, c), lambda g, i: (g, i, 0)),
        out_shape=jax.ShapeDtypeStruct((depth, 2 * l, c), BF16),
        compiler_params=_cparams(("arbitrary", "arbitrary")),
        name="hyena_filter",
    )(z2.T, t2, tp(w1p), col(b1), tp(w2), col(b2), tp(w3), col(b3), col(freq), w_out, deltas)


def _dft_consts(l, n2):
    n = 2 * l
    n1 = n // n2
    nin = n1 // 2

    def cs(k, m, period):
        ph = 2.0 * np.pi * ((np.outer(k, m)) % period) / period
        return np.cos(ph), np.sin(ph)

    k1 = np.arange(n1)
    c, s = cs(k1, np.arange(nin), n1)
    f1 = np.block([[c, s], [-s, c]])
    cf, sf = cs(k1, np.arange(n1), n1)
    f1_real = np.concatenate([cf, -sf], axis=0)
    ci, si = cs(np.arange(nin), k1, n1)
    g1 = np.block([[ci, -si], [si, ci]]) / n
    out = dict(n1=n1, n2=n2, nin=nin,
               f1=jnp.asarray(f1, F32).astype(BF16),
               f1_real=jnp.asarray(f1_real, F32).astype(BF16),
               g1=jnp.asarray(g1, F32).astype(BF16))
    if n2 > 1:
        f1_il = np.empty_like(f1)
        f1_il[0::2], f1_il[1::2] = f1[:n1], f1[n1:]
        out["f1_il"] = jnp.asarray(f1_il, F32).astype(BF16)
        out["g1_il"] = jnp.asarray(f1_il.T / n, F32).astype(BF16)
        f1r_il = np.empty_like(f1_real)
        f1r_il[0::2], f1r_il[1::2] = f1_real[:n1], f1_real[n1:]
        out["f1_real_il"] = jnp.asarray(f1r_il, F32).astype(BF16)
        m2 = np.arange(n2)
        ph = (m2[None, None, :] * k1[:, None, None] + n1 * m2[None, None, :] * m2[None, :, None]) % n
        ph = 2.0 * np.pi * ph / n
        tr, ti = np.cos(ph), -np.sin(ph)
        t_fwd = np.concatenate([np.concatenate([tr, -ti], axis=2),
                                np.concatenate([ti, tr], axis=2)], axis=1)
        t_il = np.empty_like(t_fwd)
        t_il[:, :, 0::2], t_il[:, :, 1::2] = t_fwd[:, :, :n2], t_fwd[:, :, n2:]
        out["t_il"] = jnp.asarray(t_il, F32).astype(BF16)
    return out


def _cmm_kernel(*refs, mode, chunk, nchunk):
    a_ref, x_ref, o_ref = refs[0], refs[1], refs[-1]
    a = a_ref[...]
    hm = a.shape[0] // 2
    for cc in range(nchunk):
        sl = slice(cc * chunk, (cc + 1) * chunk)
        acc = _bdot(a, x_ref[0, :, sl])
        if mode == "kfmul":
            kf_ref = refs[2]
            xr, xi = acc[:hm], acc[hm:]
            kr, ki = kf_ref[0, :, sl], kf_ref[1, :, sl]
            o_ref[0, :hm, sl] = (xr * kr - xi * ki).astype(o_ref.dtype)
            o_ref[0, hm:, sl] = (xr * ki + xi * kr).astype(o_ref.dtype)
        elif mode == "epi":
            u_ref, m_ref, b_ref = refs[2], refs[3], refs[4]
            u = u_ref[0, :, sl].astype(F32)
            o = (acc + b_ref[:, sl] * u) * m_ref[0, :, sl].astype(F32)
            o_ref[0, :, sl] = o.astype(o_ref.dtype)
        else:
            o_ref[0, :, sl] = acc.astype(o_ref.dtype)


def _cmm(a, x, mode="plain", extra=(), out_dtype=BF16):
    mr, k = a.shape
    g, _, cols = x.shape
    tc = min(8192, cols)
    chunk = min(1024, tc)
    in_specs = [pl.BlockSpec((mr, k), lambda i, j: (0, 0)),
                pl.BlockSpec((1, k, tc), lambda i, j: (i, 0, j))]
    if mode == "kfmul":
        in_specs += [pl.BlockSpec((2, mr // 2, tc), lambda i, j: (0, 0, j))]
    elif mode == "epi":
        in_specs += [pl.BlockSpec((1, mr, tc), lambda i, j: (i, 0, j)),
                     pl.BlockSpec((1, mr, tc), lambda i, j: (i, 0, j)),
                     pl.BlockSpec((1, tc), lambda i, j: (0, j))]
    return pl.pallas_call(
        functools.partial(_cmm_kernel, mode=mode, chunk=chunk, nchunk=tc // chunk),
        grid=(g, cols // tc),
        in_specs=in_specs,
        out_specs=pl.BlockSpec((1, mr, tc), lambda i, j: (i, 0, j)),
        out_shape=jax.ShapeDtypeStruct((g, mr, cols), out_dtype),
        compiler_params=_cparams(("arbitrary", "arbitrary")),
        name="dft_" + mode,
    )(a, x, *extra)


def _filter_fft_kernel(x_ref, f1_ref, t_ref, o_ref, s1, ab, *, n1, n2, nf):
    lanes = lambda f: jnp.concatenate([f(s) for s in range(nf)], axis=1)

    for s in range(nf):
        def fill(i, carry, s=s):
            src = pl.ds(pl.multiple_of(i * n2, n2), n2)
            s1[s, pl.ds(pl.multiple_of(i * FFT_PITCH, 8), n2), :] = x_ref[s, src, :].astype(F32)
            return carry
        lax.fori_loop(0, n1, fill, 0, unroll=8)

    def stage1(j, carry):
        rows = pl.ds(j, n1, stride=FFT_PITCH)
        z = lanes(lambda s: s1[s, rows, :]).astype(BF16)
        a = pltpu.bitcast(_bdot(f1_ref[...], z).astype(BF16), jnp.uint32)
        for s in range(nf):
            ab[s, rows, :] = a[:, s * LANES:(s + 1) * LANES]
        return carry

    lax.fori_loop(0, n2, stage1, 0, unroll=FFT_UNROLL)

    def mid(k, carry):
        rows = pl.ds(pl.multiple_of(k * FFT_PITCH, 8), n2)
        a = pltpu.bitcast(lanes(lambda s: ab[s, rows, :]), BF16)
        x = _bdot(t_ref[k], a).astype(BF16)
        for s in range(nf):
            o_ref[s, 0, k] = x[:n2, s * LANES:(s + 1) * LANES]
            o_ref[s, 1, k] = x[n2:, s * LANES:(s + 1) * LANES]
        return carry

    lax.fori_loop(0, n1, mid, 0, unroll=FFT_UNROLL)


def _filter_spectrum(taps, consts):
    nf, n, c = taps.shape
    n1, n2 = consts["n1"], consts["n2"]
    if n2 == 1:
        return _cmm(consts["f1_real"], taps, out_dtype=F32).reshape(nf, 2, n1, c)
    const = lambda shape: pl.BlockSpec(shape, lambda j: (0,) * len(shape))
    return pl.pallas_call(
        functools.partial(_filter_fft_kernel, n1=n1, n2=n2, nf=nf),
        grid=(c // LANES,),
        in_specs=[pl.BlockSpec((nf, n, LANES), lambda j: (0, 0, j)),
                  const((2 * n1, n1)), const((n1, 2 * n2, 2 * n2))],
        out_specs=pl.BlockSpec((nf, 2, n1, n2, LANES), lambda j: (0, 0, 0, 0, j)),
        out_shape=jax.ShapeDtypeStruct((nf, 2, n1, n2, c), BF16),
        scratch_shapes=[pltpu.VMEM((nf, n1 * FFT_PITCH, LANES), F32),
                        pltpu.VMEM((nf, n1 * FFT_PITCH, LANES), jnp.uint32)],
        compiler_params=_cparams(("arbitrary",)),
        name="filter_fft",
    )(taps, consts["f1_real_il"], consts["t_il"])


def _conv_kernel(u_ref, m_ref, hb_ref, kf_ref, f1_ref, t_ref, g1_ref, o_ref, s1, ab, *, n1, n2, nb):
    nin = n1 // 2
    npair = nb // 2
    lanes = lambda f: jnp.concatenate([f(s) for s in range(npair)], axis=1)

    for bi in range(nb):
        def fill(i, carry, bi=bi):
            src = pl.ds(pl.multiple_of(i * n2, n2), n2)
            dst = pl.ds(pl.multiple_of(i * FFT_PITCH, 8), n2)
            s1[bi % 2, bi // 2, dst, :] = u_ref[bi, src, :].astype(F32)
            return carry
        lax.fori_loop(0, nin, fill, 0, unroll=8)

    def stage1(j, carry):
        rows = pl.ds(j, nin, stride=FFT_PITCH)
        z = jnp.concatenate([lanes(lambda s: s1[0, s, rows, :]),
                             lanes(lambda s: s1[1, s, rows, :])], axis=0).astype(BF16)
        a = pltpu.bitcast(_bdot(f1_ref[...], z).astype(BF16), jnp.uint32)
        for s in range(npair):
            ab[s, pl.ds(j, n1, stride=FFT_PITCH), :] = a[:, s * LANES:(s + 1) * LANES]
        return carry

    lax.fori_loop(0, n2, stage1, 0, unroll=FFT_UNROLL)

    def mid(k, carry):
        rows = pl.ds(pl.multiple_of(k * FFT_PITCH, 8), n2)
        a = pltpu.bitcast(lanes(lambda s: ab[s, rows, :]), BF16)
        t = t_ref[k]
        x = _bdot(t, a)
        xr, xi = x[:n2], x[n2:]
        kr = lanes(lambda s: kf_ref[0, k].astype(F32))
        ki = lanes(lambda s: kf_ref[1, k].astype(F32))
        y = jnp.concatenate([xr * kr - xi * ki, xr * ki + xi * kr], axis=0).astype(BF16)
        bm = lax.dot_general(t, y, (((0,), (0,)), ((), ())), preferred_element_type=F32)
        w = pltpu.bitcast(bm.astype(BF16), jnp.uint32)
        for s in range(npair):
            ab[s, rows, :] = w[:, s * LANES:(s + 1) * LANES]
        return carry

    lax.fori_loop(0, n1, mid, 0, unroll=2 * FFT_UNROLL)

    def stage4(j, carry):
        b = pltpu.bitcast(lanes(lambda s: ab[s, pl.ds(j, n1, stride=FFT_PITCH), :]), BF16)
        y = _bdot(g1_ref[...], b)
        rows = pl.ds(j, nin, stride=FFT_PITCH)
        for s in range(npair):
            s1[0, s, rows, :] = y[:nin, s * LANES:(s + 1) * LANES]
            s1[1, s, rows, :] = y[nin:, s * LANES:(s + 1) * LANES]
        return carry

    lax.fori_loop(0, n2, stage4, 0, unroll=FFT_UNROLL)

    for bi in range(nb):
        def emit(i, carry, bi=bi):
            dst = pl.ds(pl.multiple_of(i * n2, n2), n2)
            src = pl.ds(pl.multiple_of(i * FFT_PITCH, 8), n2)
            y = s1[bi % 2, bi // 2, src, :] + hb_ref[...] * u_ref[bi, dst, :].astype(F32)
            o_ref[bi, dst, :] = (y * m_ref[bi, dst, :].astype(F32)).astype(BF16)
            return carry
        lax.fori_loop(0, nin, emit, 0, unroll=8)


def _long_conv(u, m, bias, kf, layer, consts):
    b, l, c = u.shape
    n1, n2, nin = consts["n1"], consts["n2"], consts["nin"]
    p = b // 2
    if n2 == 1:
        x = u.reshape(p, 2 * nin, c)
        y = _cmm(consts["f1"], x, mode="kfmul", extra=(kf[layer],))
        extra = (x, m.reshape(p, 2 * nin, c), bias.reshape(1, c))
        return _cmm(consts["g1"], y, mode="epi", extra=extra).reshape(b, l, c)
    const = lambda shape: pl.BlockSpec(shape, lambda j: (0,) * len(shape))
    return pl.pallas_call(
        functools.partial(_conv_kernel, n1=n1, n2=n2, nb=b),
        grid=(c // LANES,),
        in_specs=[pl.BlockSpec((b, l, LANES), lambda j: (0, 0, j)),
                  pl.BlockSpec((b, l, LANES), lambda j: (0, 0, j)),
                  pl.BlockSpec((1, LANES), lambda j: (0, j)),
                  pl.BlockSpec((None, 2, n1, n2, LANES), lambda j: (layer, 0, 0, 0, j)),
                  const((2 * n1, n1)), const((n1, 2 * n2, 2 * n2)), const((n1, 2 * n1))],
        out_specs=pl.BlockSpec((b, l, LANES), lambda j: (0, 0, j)),
        out_shape=jax.ShapeDtypeStruct((b, l, c), BF16),
        scratch_shapes=[pltpu.VMEM((2, p, nin * FFT_PITCH, LANES), F32),
                        pltpu.VMEM((p, n1 * FFT_PITCH, LANES), jnp.uint32)],
        compiler_params=_cparams(("arbitrary",)),
        name="long_conv",
    )(u, m, bias.reshape(1, c), kf, consts["f1_il"], consts["t_il"], consts["g1_il"])


def _ret_kernel(*refs, nch, cl, states_only):
    if states_only:
        rd_ref, k_ref, v_ref, s0f_ref, s0b_ref, sf_ref, sb_ref, sf_scr, sb_scr = refs
    else:
        (rd_ref, q_ref, k_ref, v_ref, g_ref, s0f_ref, s0b_ref,
         o_ref, sf_ref, sb_ref, sf_scr, sb_scr, sfall_scr, sball_scr, dmask_scr) = refs
    hd = pl.program_id(1)
    one = jnp.ones((1, 1), F32)
    lgf = -jnp.exp(one * rd_ref[0, hd])
    lgb = -jnp.exp(one * rd_ref[1, hd])
    ri = lax.broadcasted_iota(jnp.int32, (cl, 1), 0).astype(F32)
    ci = lax.broadcasted_iota(jnp.int32, (1, cl), 1).astype(F32)
    per_token = lambda dec: jnp.broadcast_to(dec, (cl, QK_DIM)).astype(BF16)
    k_dec_f = per_token(jnp.exp(lgf * (cl - 1.0 - ri)))
    k_dec_b = per_token(jnp.exp(lgb * ri))
    chunk_dec_f = jnp.exp(lgf * cl)
    chunk_dec_b = jnp.exp(lgb * cl)

    def kv_outer(k, dec, v):
        return lax.dot_general(k * dec, v, (((0,), (0,)), ((), ())), preferred_element_type=F32)

    def load(ref, c):
        return ref[0, 0, pl.ds(pl.multiple_of(c * cl, cl), cl), :]

    sf_scr[...] = s0f_ref[0, 0]
    sb_scr[...] = s0b_ref[0, 0]

    def sweep(t, carry):
        cb = nch - 1 - t
        if not states_only:
            sfall_scr[t] = sf_scr[...].astype(BF16)
            sball_scr[cb] = sb_scr[...].astype(BF16)
        sf_scr[...] = (sf_scr[...] * chunk_dec_f
                       + kv_outer(load(k_ref, t), k_dec_f, load(v_ref, t)))
        sb_scr[...] = (sb_scr[...] * chunk_dec_b
                       + kv_outer(load(k_ref, cb), k_dec_b, load(v_ref, cb)))
        return carry

    lax.fori_loop(0, nch, sweep, 0, unroll=2 * RET_UNROLL)
    sf_ref[0, 0] = sf_scr[...]
    sb_ref[0, 0] = sb_scr[...]
    if states_only:
        return

    q_dec_f = per_token(jnp.exp(lgf * (ri + 1.0)))
    q_dec_b = per_token(jnp.exp(lgb * (cl - ri)))
    diff = ri - ci
    dmask_scr[...] = (jnp.where(diff >= 0, jnp.exp(lgf * jnp.maximum(diff, 0.0)), 0.0)
                      + jnp.where(diff <= 0, jnp.exp(lgb * jnp.maximum(-diff, 0.0)), 0.0))

    def emit(c, carry):
        q = load(q_ref, c)
        k = load(k_ref, c)
        v = load(v_ref, c)
        scores = lax.dot_general(q, k, (((1,), (1,)), ((), ())), preferred_element_type=F32)
        o = _bdot((scores * dmask_scr[...]).astype(BF16), v)
        q2 = jnp.concatenate([q * q_dec_f, q * q_dec_b], axis=1)
        s2 = jnp.concatenate([sfall_scr[c], sball_scr[c]], axis=0)
        o = o + _bdot(q2, s2)
        o = o * lax.rsqrt(jnp.mean(o * o, axis=-1, keepdims=True) + NORM_EPS)
        r0 = pl.multiple_of(c * cl, cl)
        o_ref[0, pl.ds(r0, cl), :] = o.astype(BF16) * load(g_ref, c)
        return carry

    lax.fori_loop(0, nch, emit, 0, unroll=2 * RET_UNROLL)


def _retention(qk, k_off, v, gate, ret_decay, s0f, s0b, b, states_only=False):
    h = v.shape[0]
    l = v.shape[1] // b
    cl = min(RET_CHUNK, l)
    nch = l // cl
    sspec = pl.BlockSpec((1, 1, QK_DIM, V_DIM), lambda i, j: (i, j, 0, 0))
    state_shape = jax.ShapeDtypeStruct((b, h, QK_DIM, V_DIM), F32)
    hspec = lambda w, off: pl.BlockSpec((1, 1, l, w), lambda i, j: (off + j, i, 0, 0))
    qk4 = qk.reshape(k_off + h, b, l, QK_DIM)
    v4 = v.reshape(h, b, l, V_DIM)
    smem = pl.BlockSpec(memory_space=pltpu.SMEM)
    kern = functools.partial(_ret_kernel, nch=nch, cl=cl, states_only=states_only)
    if states_only:
        return pl.pallas_call(
            kern, grid=(b, h),
            in_specs=[smem, hspec(QK_DIM, k_off), hspec(V_DIM, 0), sspec, sspec],
            out_specs=[sspec, sspec],
            out_shape=[state_shape, state_shape],
            scratch_shapes=[pltpu.VMEM((QK_DIM, V_DIM), F32)] * 2,
            compiler_params=_cparams(("arbitrary", "arbitrary")),
            name="retention_states",
        )(ret_decay, qk4, v4, s0f, s0b)
    return pl.pallas_call(
        kern, grid=(b, h),
        in_specs=[smem, hspec(QK_DIM, 0), hspec(QK_DIM, k_off), hspec(V_DIM, 0), hspec(V_DIM, 0),
                  sspec, sspec],
        out_specs=[pl.BlockSpec((1, l, V_DIM), lambda i, j: (i, 0, j)), sspec, sspec],
        out_shape=[jax.ShapeDtypeStruct((b, l, h * V_DIM), BF16), state_shape, state_shape],
        scratch_shapes=[pltpu.VMEM((QK_DIM, V_DIM), F32),
                        pltpu.VMEM((QK_DIM, V_DIM), F32),
                        pltpu.VMEM((nch, QK_DIM, V_DIM), BF16),
                        pltpu.VMEM((nch, QK_DIM, V_DIM), BF16),
                        pltpu.VMEM((cl, cl), F32)],
        compiler_params=_cparams(("arbitrary", "arbitrary")),
        name="retention",
    )(ret_decay, qk4, qk4, v4, gate.reshape(h, b, l, V_DIM), s0f, s0b)


def _merge_kernel(hy_ref, ret_ref, why_ref, wret_ref, ghy_ref, gret_ref, o_ref):
    a = _bdot(hy_ref[...], why_ref[...])
    b = _bdot(ret_ref[...], wret_ref[...])
    o_ref[...] = (ghy_ref[...].astype(F32) * a + gret_ref[...].astype(F32) * b).astype(BF16)


def _merge(hy, ret, pmg, w_hy, w_ret):
    mm, d = hy.shape
    tm = min(512, mm)
    tn = min(1024, d)
    nb = d // tn
    return pl.pallas_call(
        _merge_kernel,
        grid=(mm // tm, nb),
        in_specs=[
            pl.BlockSpec((tm, d), lambda i, j: (i, 0)),
            pl.BlockSpec((tm, 2 * d), lambda i, j: (i, 0)),
            pl.BlockSpec((d, tn), lambda i, j: (0, j)),
            pl.BlockSpec((2 * d, tn), lambda i, j: (0, j)),
            pl.BlockSpec((tm, tn), lambda i, j: (i, j)),
            pl.BlockSpec((tm, tn), lambda i, j: (i, nb + j)),
        ],
        out_specs=pl.BlockSpec((tm, tn), lambda i, j: (i, j)),
        out_shape=jax.ShapeDtypeStruct((mm, d), BF16),
        compiler_params=_cparams(("arbitrary", "arbitrary")),
        name="merge_proj",
    )(hy, ret, w_hy, w_ret, pmg, pmg)


def _wo_kernel(*refs, mode):
    if mode == "final":
        m_ref, w_ref, x_ref, g_ref, fg_ref, o_ref = refs
    else:
        m_ref, w_ref, x_ref, g_ref, ng_ref, nsc_ref, nsh_ref, o_ref, h_ref = refs
    xn = x_ref[0] + g_ref[0] * _bdot(m_ref[0], w_ref[...])
    y = xn * lax.rsqrt(jnp.mean(xn * xn, axis=-1, keepdims=True) + NORM_EPS)
    if mode == "final":
        o_ref[0] = y * fg_ref[...]
    else:
        o_ref[0] = xn
        h_ref[0] = ((y * ng_ref[...]) * (1.0 + nsc_ref[0]) + nsh_ref[0]).astype(BF16)


def _wo_residual(mrg, w_o, x, gate, final_g=None, next_norm=None):
    b, l, d = x.shape
    tl = min(512, l)
    tok = pl.BlockSpec((1, tl, d), lambda i, j: (i, j, 0))
    per_batch = pl.BlockSpec((1, 1, d), lambda i, j: (i, 0, 0))
    vec = pl.BlockSpec((1, d), lambda i, j: (0, 0))
    in_specs = [tok, pl.BlockSpec((d, d), lambda i, j: (0, 0)), tok, per_batch]
    args = [mrg.reshape(b, l, d), w_o, x, gate]
    if final_g is not None:
        mode = "final"
        in_specs.append(vec)
        args.append(final_g.reshape(1, d))
        out_specs, out_shape = tok, jax.ShapeDtypeStruct((b, l, d), F32)
    else:
        mode = "next"
        g, sc, sh = next_norm
        in_specs += [vec, per_batch, per_batch]
        args += [g.reshape(1, d), sc, sh]
        out_specs = [tok, tok]
        out_shape = [jax.ShapeDtypeStruct((b, l, d), F32), jax.ShapeDtypeStruct((b, l, d), BF16)]
    return pl.pallas_call(
        functools.partial(_wo_kernel, mode=mode),
        grid=(b, l // tl),
        in_specs=in_specs,
        out_specs=out_specs,
        out_shape=out_shape,
        compiler_params=_cparams(("arbitrary", "arbitrary")),
        name="wo_residual_" + mode,
    )(*args)


def _mixer(h, use_rope, s0f, s0b, w_in, layer, conv_w, conv_b, kf, consts, hy_bias, ret_decay,
           w_hy_b, w_ret_b):
    b, l, d = h.shape
    h2 = h.reshape(b * l, d)
    proj = functools.partial(_proj, h2, w_in, layer)
    qk = proj(0, 2 * d, l, "rope" if use_rope else "qkscale", hw=QK_DIM)
    v = proj(2 * d, 2 * d, l, "plain", hw=V_DIM)
    rg = proj(4 * d, 2 * d, l, "silu", hw=V_DIM)
    phy = proj(6 * d, 3 * d, l, "plain")
    phg = proj(9 * d, d, l, "silu")
    pmg = proj(10 * d, 2 * d, l, "sigmoid")
    u, m = _hyena_front(phy.reshape(b, l, 3 * d), phg.reshape(b, l, d), conv_w, conv_b)
    hy = _long_conv(u, m, hy_bias, kf, layer, consts)
    ret, sf, sb = _retention(qk, d // QK_DIM, v, rg, ret_decay, s0f, s0b, b)
    mrg = _merge(hy.reshape(b * l, d), ret.reshape(b * l, 2 * d), pmg, w_hy_b, w_ret_b)
    return mrg, sf, sb


def kernel(x, c, ctx, c_ctx, ln_g, ada_w, ada_b, w_in, hy_conv_w, hy_conv_b, hy_filt_w1,
           hy_filt_b1, hy_filt_w2, hy_filt_b2, hy_filt_w3, hy_filt_b3, hy_filt_freq,
           hy_filt_wout, hy_bias, ret_decay, w_hy_out, w_ret_out, w_o, final_g):
    b, l, d = x.shape
    lc = ctx.shape[1]
    depth = ln_g.shape[0]
    h = d // QK_DIM
    assert b % 2 == 0 and d % QK_DIM == 0 and l % GRID_W == 0

    rows = -(-(b + 1) // 8) * 8
    cond = jnp.zeros((rows, d), F32).at[:b].set(c).at[b].set(c_ctx)
    mod = _ada_modulation(cond, ada_w, ada_b)

    consts_l = _dft_consts(l, FFT_N2 if (2 * l) % FFT_N2 == 0 and l >= 1024 else 1)
    consts_c = _dft_consts(lc, 1)
    zero_state = jnp.zeros((b, h, QK_DIM, V_DIM), F32)
    filt = (hy_filt_w1, hy_filt_b1, hy_filt_w2, hy_filt_b2, hy_filt_w3, hy_filt_b3, hy_filt_freq,
            hy_filt_wout)
    kf_l = _filter_spectrum(_hyena_filter_taps(l, d, *filt), consts_l)
    kf_c = _filter_spectrum(_hyena_filter_taps(lc, d, *filt), consts_c)

    lat = lambda a: a[:b, None, :]
    cx = lambda a: jnp.broadcast_to(a[b][None, None, :], (b, 1, d))
    shift = lambda i: mod[i, :, :d]
    scale = lambda i: mod[i, :, d:2 * d]
    h_ctx = _prenorm(ctx, ln_g[0], cx(scale(0)), cx(shift(0)))
    h_lat = _prenorm(x, ln_g[0], lat(scale(0)), lat(shift(0)))

    for i in range(depth):
        gt = mod[i, :, 2 * d:]
        last = i == depth - 1
        w_hy_b = w_hy_out[i].astype(BF16)
        w_ret_b = w_ret_out[i].astype(BF16)
        w_o_b = w_o[i].astype(BF16)
        params = lambda kf, consts: (w_in, i, hy_conv_w[i], hy_conv_b[i], kf, consts, hy_bias[i],
                                     ret_decay[i], w_hy_b, w_ret_b)

        if not last:
            mrg_c, s_ctx_f, s_ctx_b = _mixer(h_ctx, False, zero_state, zero_state,
                                             *params(kf_c, consts_c))
            ctx, h_ctx = _wo_residual(mrg_c, w_o_b, ctx, cx(gt),
                                      next_norm=(ln_g[i + 1], cx(scale(i + 1)), cx(shift(i + 1))))
        else:
            hc2 = h_ctx.reshape(b * lc, d)
            k_c = _proj(hc2, w_in, i, d, d, lc, "qkscale", hw=QK_DIM)
            v_c = _proj(hc2, w_in, i, 2 * d, 2 * d, lc, "plain", hw=V_DIM)
            s_ctx_f, s_ctx_b = _retention(k_c, 0, v_c, None, ret_decay[i], zero_state, zero_state,
                                          b, states_only=True)

        mrg, _, _ = _mixer(h_lat, True, s_ctx_f, s_ctx_b, *params(kf_l, consts_l))
        if last:
            x = _wo_residual(mrg, w_o_b, x, lat(gt), final_g=final_g)
        else:
            x, h_lat = _wo_residual(mrg, w_o_b, x, lat(gt),
                                    next_norm=(ln_g[i + 1], lat(scale(i + 1)), lat(shift(i + 1))))

    return x
```
